```python
import jax
import jax.numpy as jnp
from jax import lax
import numpy as np

D_MODEL = 1024
BATCH = 16
SEQ = 4096
DEPTH = 4

N_A_LAYERS = DEPTH // 2
N_B_LAYERS = DEPTH - N_A_LAYERS
EPS = 1e-6

GLA_HEADS = 4
GLA_DK = D_MODEL // 2
GLA_DV = D_MODEL
GLA_HK = GLA_DK // GLA_HEADS
GLA_HV = GLA_DV // GLA_HEADS
GLA_GATE_RANK = 16
GLA_GATE_NORM = 16.0
GLA_CHUNK = 64
GLA_IN = 2 * GLA_DK + 2 * GLA_DV + GLA_GATE_RANK

FOX_HD = 64
FOX_HEADS = D_MODEL // FOX_HD
FOX_QBLOCK = 128
FOX_FGATE_BIAS = 2.0

PEER_HEADS = 8
PEER_NKEYS = 128
PEER_EXPERTS = PEER_NKEYS * PEER_NKEYS
PEER_TOPK = 16
PEER_DQ = 256
PEER_TOKBLOCK = 128

kernel_name = "yoco_gla_fox_peer_adaln"

F32 = jnp.float32


def _rms(x):
    x32 = x.astype(F32)
    return x32 * lax.rsqrt(jnp.mean(x32 * x32, axis=-1, keepdims=True) + EPS)


def rmsnorm_gain(x, g):
    return (_rms(x) * g.astype(F32)).astype(x.dtype)


def adaln(x, shift, scale):
    y = _rms(x) * (1.0 + scale[:, None, :].astype(F32)) + shift[:, None, :].astype(F32)
    return y.astype(x.dtype)


def gla_mixer(h, w_in, w_gate2, b_gate, o_norm, w_out):
    B, S, _ = h.shape
    C, H = GLA_CHUNK, GLA_HEADS
    nc = S // C
    proj = h @ w_in
    q, k, v, g, glow = jnp.split(
        proj, [GLA_DK, 2 * GLA_DK, 2 * GLA_DK + GLA_DV, 2 * GLA_DK + 2 * GLA_DV], axis=-1)
    gk = jax.nn.log_sigmoid((glow @ w_gate2 + b_gate).astype(F32)) / GLA_GATE_NORM

    def heads(t, hd):
        return t.reshape(B, nc, C, H, hd).transpose(1, 0, 3, 2, 4)

    qc = heads(q * (GLA_HK ** -0.5), GLA_HK)
    kc = heads(k, GLA_HK)
    vc = heads(v, GLA_HV)
    gc = heads(gk, GLA_HK)
    causal = jnp.tril(jnp.ones((C, C), dtype=bool))

    def step(state, inp):
        qb, kb, vb, gb = inp
        b = jnp.cumsum(gb, axis=-2)
        qf = qb.astype(F32)
        kf = kb.astype(F32)
        vf = vb.astype(F32)
        o_inter = jnp.einsum('bhtk,bhkv->bhtv', qf * jnp.exp(b), state)
        rel = jnp.where(causal[:, :, None], b[..., :, None, :] - b[..., None, :, :], -jnp.inf)
        attn = jnp.einsum('bhtk,bhsk,bhtsk->bhts', qf, kf, jnp.exp(rel))
        o = o_inter + jnp.einsum('bhts,bhsv->bhtv', attn, vf)
        b_last = b[..., -1:, :]
        state = (jnp.exp(b[..., -1, :])[..., None] * state
                 + jnp.einsum('bhsk,bhsv->bhkv', kf * jnp.exp(b_last - b), vf))
        return state, o.astype(vb.dtype)

    state0 = jnp.zeros((B, H, GLA_HK, GLA_HV), F32)
    _, o = lax.scan(step, state0, (qc, kc, vc, gc))
    o = o.transpose(1, 0, 3, 2, 4).reshape(B, S, H, GLA_HV)
    o = rmsnorm_gain(o, o_norm) * jax.nn.silu(g.reshape(B, S, H, GLA_HV))
    return o.reshape(B, S, GLA_DV) @ w_out


def fox_shared_kv(hk, w_kvf, b_f, k_norm):
    B, S, _ = hk.shape
    proj = hk @ w_kvf
    k, v, f = jnp.split(proj, [D_MODEL, 2 * D_MODEL], axis=-1)
    k = rmsnorm_gain(k.reshape(B, S, FOX_HEADS, FOX_HD), k_norm).transpose(0, 2, 1, 3)
    v = v.reshape(B, S, FOX_HEADS, FOX_HD).transpose(0, 2, 1, 3)
    log_f = jax.nn.log_sigmoid(f.astype(F32) + b_f.astype(F32))
    cum = jnp.cumsum(log_f, axis=1).transpose(0, 2, 1)
    return k, v, cum


def fox_mixer(h, k, v, cum, w_qg, q_norm, w_out):
    B, S, _ = h.shape
    QB = FOX_QBLOCK
    q, g = jnp.split(h @ w_qg, [D_MODEL], axis=-1)
    q = rmsnorm_gain(q.reshape(B, S, FOX_HEADS, FOX_HD), q_norm).transpose(0, 2, 1, 3)
    scale = FOX_HD ** -0.5
    diag = jnp.tril(jnp.ones((QB, QB), dtype=bool))
    outs = []
    for i in range(S // QB):
        lo, hi = i * QB, (i + 1) * QB
        s = jnp.einsum('bhqd,bhkd->bhqk', q[:, :, lo:hi], k[:, :, :hi]).astype(F32) * scale
        s = s + cum[:, :, lo:hi, None] - cum[:, :, None, :hi]
        mask = jnp.concatenate([jnp.ones((QB, lo), dtype=bool), diag], axis=1)
        p = jax.nn.softmax(jnp.where(mask, s, -jnp.inf), axis=-1)
        outs.append(jnp.einsum('bhqk,bhkd->bhqd', p.astype(v.dtype), v[:, :, :hi]))
    o = jnp.concatenate(outs, axis=2).transpose(0, 2, 1, 3).reshape(B, S, D_MODEL)
    o = o * jax.nn.sigmoid(g)
    return o @ w_out


def peer_ffn(h, w_q, sub_keys, u_tab, v_tab):
    B, S, D = h.shape
    TB, K, NK, PH = PEER_TOKBLOCK, PEER_TOPK, PEER_NKEYS, PEER_HEADS
    xt = h.reshape((B * S) // TB, TB, D)

    def block(xb):
        q = (xb @ w_q).reshape(TB, PH, 2, PEER_DQ // 2)
        s = jnp.einsum('thpd,hpnd->thpn', q, sub_keys).astype(F32)
        s_half, i_half = lax.top_k(s, K)
        cand = (s_half[:, :, 0, :, None] + s_half[:, :, 1, None, :]).reshape(TB, PH, K * K)
        top_s, top_c = lax.top_k(cand, K)
        i0 = jnp.take_along_axis(i_half[:, :, 0], top_c // K, axis=-1)
        i1 = jnp.take_along_axis(i_half[:, :, 1], top_c % K, axis=-1)
        idx = i0 * NK + i1
        gate = jax.nn.softmax(top_s, axis=-1)
        a = jax.nn.gelu(jnp.einsum('td,thkd->thk', xb, u_tab[idx]).astype(F32), approximate=False)
        w = (gate * a).astype(xb.dtype)
        return jnp.einsum('thk,thkd->td', w, v_tab[idx])

    return lax.map(block, xt).reshape(B, S, D)


def setup_inputs(seed: int = 0) -> dict:
    key = jax.random.key(seed)
    ks = jax.random.split(key, 21)

    def nrm(k, shape, scale):
        return jax.random.normal(k, shape, F32) * scale

    NA, NB, D = N_A_LAYERS, N_B_LAYERS, D_MODEL
    return {
        'x': nrm(ks[0], (BATCH, SEQ, D), 1.0),
        'c': nrm(ks[1], (BATCH, D), 1.0),
        'mod_w': nrm(ks[2], (DEPTH, D, 6 * D), 0.5 * D ** -0.5),
        'mod_b': nrm(ks[3], (DEPTH, 6 * D), 0.02),
        'gla_w_in': nrm(ks[4], (NA, D, GLA_IN), D ** -0.5),
        'gla_w_gate2': nrm(ks[5], (NA, GLA_GATE_RANK, GLA_DK), GLA_GATE_RANK ** -0.5),
        'gla_b_gate': nrm(ks[6], (NA, GLA_DK), 0.02),
        'gla_o_norm': 1.0 + nrm(ks[7], (NA, GLA_HV), 0.02),
        'gla_w_out': nrm(ks[8], (NA, GLA_DV, D), GLA_DV ** -0.5),
        'kv_mod_w': nrm(ks[9], (D, 2 * D), 0.5 * D ** -0.5),
        'kv_mod_b': nrm(ks[10], (2 * D,), 0.02),
        'fox_w_kvf': nrm(ks[11], (D, 2 * D + FOX_HEADS), D ** -0.5),
        'fox_b_f': FOX_FGATE_BIAS + nrm(ks[12], (FOX_HEADS,), 0.1),
        'fox_k_norm': 1.0 + nrm(ks[13], (FOX_HD,), 0.02),
        'fox_w_qg': nrm(ks[14], (NB, D, 2 * D), D ** -0.5),
        'fox_q_norm': 1.0 + nrm(ks[15], (NB, FOX_HD), 0.02),
        'fox_w_out': nrm(ks[16], (NB, D, D), D ** -0.5),
        'peer_w_q': nrm(ks[17], (DEPTH, D, PEER_HEADS * PEER_DQ), D ** -0.5),
        'peer_sub_keys': nrm(ks[18], (DEPTH, PEER_HEADS, 2, PEER_NKEYS, PEER_DQ // 2), (PEER_DQ // 2) ** -0.5),
        'peer_u': nrm(ks[19], (DEPTH, PEER_EXPERTS, D), D ** -0.5),
        'peer_v': nrm(ks[20], (DEPTH, PEER_EXPERTS, D), PEER_HEADS ** -0.5),
    }


def reference(x, c, mod_w, mod_b, gla_w_in, gla_w_gate2, gla_b_gate, gla_o_norm, gla_w_out,
              kv_mod_w, kv_mod_b, fox_w_kvf, fox_b_f, fox_k_norm, fox_w_qg, fox_q_norm, fox_w_out,
              peer_w_q, peer_sub_keys, peer_u, peer_v):
    shared = None
    for layer in range(DEPTH):
        if layer == N_A_LAYERS:
            kv_shift, kv_scale = jnp.split(c @ kv_mod_w + kv_mod_b, 2, axis=-1)
            shared = fox_shared_kv(adaln(x, kv_shift, kv_scale), fox_w_kvf, fox_b_f, fox_k_norm)
        mod = c @ mod_w[layer] + mod_b[layer]
        sh1, sc1, g1, sh2, sc2, g2 = jnp.split(mod, 6, axis=-1)
        h = adaln(x, sh1, sc1)
        if layer < N_A_LAYERS:
            mix = gla_mixer(h, gla_w_in[layer], gla_w_gate2[layer], gla_b_gate[layer],
                            gla_o_norm[layer], gla_w_out[layer])
        else:
            j = layer - N_A_LAYERS
            k_sh, v_sh, cum_sh = shared
            mix = fox_mixer(h, k_sh, v_sh, cum_sh, fox_w_qg[j], fox_q_norm[j], fox_w_out[j])
        x = x + g1[:, None, :] * mix
        h = adaln(x, sh2, sc2)
        x = x + g2[:, None, :] * peer_ffn(h, peer_w_q[layer], peer_sub_keys[layer],
                                          peer_u[layer], peer_v[layer])
    return x
```

```python
import functools
import math

import jax
import jax.numpy as jnp
from jax import lax
from jax.experimental import pallas as pl
from jax.experimental.pallas import tpu as pltpu

F32 = jnp.float32
BF16 = jnp.bfloat16
HI = lax.Precision.HIGHEST
EPS = 1e-6
NEG_INF = float("-inf")

VMEM_LIMIT_BYTES = 48 * 1024 * 1024
LANES = 128

GLA_HEADS = 4
GLA_HK = 128
GLA_HV = 256
GLA_GATE_RANK = 16
GLA_GATE_NORM = 16.0
GLA_CHUNK = 64

FOX_HD = 64
FOX_HEADS = 16

PEER_HEADS = 8
PEER_NKEYS = 128
PEER_TOPK = 16


def _params(*sem):
    return pltpu.CompilerParams(dimension_semantics=sem, vmem_limit_bytes=VMEM_LIMIT_BYTES)


def _adaln_rows(x, shift, scale):
    ms = jnp.mean(x * x, axis=-1, keepdims=True)
    return x * lax.rsqrt(ms + EPS) * (1.0 + scale) + shift


def _mod_kernel(c_ref, w_ref, b_ref, o_ref):
    o_ref[...] = jnp.dot(c_ref[...], w_ref[...], preferred_element_type=F32, precision=HI) + b_ref[...]


def mod_matmul(c, w, b, tn=512):
    bsz, d = c.shape
    n = w.shape[1]
    return pl.pallas_call(
        _mod_kernel,
        grid=(n // tn,),
        in_specs=[
            pl.BlockSpec((bsz, d), lambda j: (0, 0)),
            pl.BlockSpec((d, tn), lambda j: (0, j)),
            pl.BlockSpec((1, tn), lambda j: (0, j)),
        ],
        out_specs=pl.BlockSpec((bsz, tn), lambda j: (0, j)),
        out_shape=jax.ShapeDtypeStruct((bsz, n), F32),
        compiler_params=_params("parallel"),
        name="mod_matmul",
    )(c, w, b.reshape(1, n))


def _adaln_mm_kernel(x_ref, sh_ref, sc_ref, w_ref, o_ref, h_ref):
    @pl.when(pl.program_id(1) == 0)
    def _():
        h_ref[...] = _adaln_rows(x_ref[...], sh_ref[0], sc_ref[0]).astype(h_ref.dtype)

    if w_ref.dtype == BF16:
        acc = jnp.dot(h_ref[...], w_ref[...], preferred_element_type=F32)
    else:
        acc = jnp.dot(h_ref[...], w_ref[...], preferred_element_type=F32, precision=HI)
    o_ref[...] = acc.astype(o_ref.dtype)


def adaln_matmul(xf, shift, scale, w, seq, tm=512, tn=512):
    t, d = xf.shape
    n = w.shape[1]
    tn = min(tn, n)
    per_seq = seq // tm
    return pl.pallas_call(
        _adaln_mm_kernel,
        grid=(t // tm, n // tn),
        in_specs=[
            pl.BlockSpec((tm, d), lambda i, j: (i, 0)),
            pl.BlockSpec((1, 1, d), lambda i, j: (i // per_seq, 0, 0)),
            pl.BlockSpec((1, 1, d), lambda i, j: (i // per_seq, 0, 0)),
            pl.BlockSpec((d, tn), lambda i, j: (0, j)),
        ],
        out_specs=pl.BlockSpec((tm, tn), lambda i, j: (i, j)),
        out_shape=jax.ShapeDtypeStruct((t, n), F32),
        scratch_shapes=[pltpu.VMEM((tm, d), w.dtype)],
        compiler_params=_params("parallel", "arbitrary"),
        name="adaln_matmul",
    )(xf, shift, scale, w)


def _gla_kernel(q_ref, k_ref, v_ref, gl_ref, w2_ref, bg_ref, o_ref, st_ref, b_scr, k_scr, *, nchunk):
    c = GLA_CHUNK

    @pl.when(pl.program_id(2) == 0)
    def _():
        st_ref[...] = jnp.zeros_like(st_ref)

    row = lax.broadcasted_iota(jnp.int32, (c, GLA_HK), 0)
    lane = lax.broadcasted_iota(jnp.int32, (c, GLA_HK), 1)
    tri = (lax.broadcasted_iota(jnp.int32, (c, c), 1) <= lax.broadcasted_iota(jnp.int32, (c, c), 0)).astype(F32)

    def chunk(ci, carry):
        r0 = pl.multiple_of(ci * c, c)
        q = q_ref[pl.ds(r0, c), :] * (GLA_HK ** -0.5)
        k = k_ref[pl.ds(r0, c), :]
        v = v_ref[pl.ds(r0, c), :]
        pre = jnp.dot(gl_ref[pl.ds(r0, c), :], w2_ref[...], preferred_element_type=F32, precision=HI) + bg_ref[...]
        gk = jax.nn.log_sigmoid(pre) * (1.0 / GLA_GATE_NORM)
        b = jnp.dot(tri, gk, preferred_element_type=F32, precision=HI)
        b_scr[...] = b
        k_scr[...] = k
        st = st_ref[...]
        o_inter = lax.dot_general(q * jnp.exp(b), st, (((1,), (1,)), ((), ())),
                                  preferred_element_type=F32, precision=HI)

        def col(si, attn):
            bs = b_scr[pl.ds(si, 1), :]
            ks = k_scr[pl.ds(si, 1), :]
            rel = jnp.where(row >= si, b - bs, NEG_INF)
            m = jnp.exp(rel) * (q * ks)
            return jnp.where(lane == si, jnp.sum(m, axis=-1, keepdims=True), attn)

        attn = lax.fori_loop(0, c, col, jnp.zeros((c, GLA_HK), F32), unroll=4)
        o = o_inter + jnp.dot(attn[:, :c], v, preferred_element_type=F32, precision=HI)
        o_ref[pl.ds(r0, c), :] = o
        b_last = b[c - 1:c, :]
        kdec = k * jnp.exp(b_last - b)
        st_ref[...] = st * jnp.exp(b_last) + lax.dot_general(
            v, kdec, (((0,), (0,)), ((), ())), preferred_element_type=F32, precision=HI)
        return carry

    lax.fori_loop(0, nchunk, chunk, 0)


def gla_recurrence(proj, glow, w2p, bg, bsz, seq, ts=512):
    t = proj.shape[0]
    ns = seq // ts
    kcol = (GLA_HEADS * GLA_HK) // GLA_HK
    vcol = (2 * GLA_HEADS * GLA_HK) // GLA_HV
    return pl.pallas_call(
        functools.partial(_gla_kernel, nchunk=ts // GLA_CHUNK),
        grid=(bsz, GLA_HEADS, ns),
        in_specs=[
            pl.BlockSpec((ts, GLA_HK), lambda b, h, s: (b * ns + s, h)),
            pl.BlockSpec((ts, GLA_HK), lambda b, h, s: (b * ns + s, kcol + h)),
            pl.BlockSpec((ts, GLA_HV), lambda b, h, s: (b * ns + s, vcol + h)),
            pl.BlockSpec((ts, LANES), lambda b, h, s: (b * ns + s, 0)),
            pl.BlockSpec((LANES, GLA_HK), lambda b, h, s: (0, h)),
            pl.BlockSpec((1, GLA_HK), lambda b, h, s: (0, h)),
        ],
        out_specs=pl.BlockSpec((ts, GLA_HV), lambda b, h, s: (b * ns + s, h)),
        out_shape=jax.ShapeDtypeStruct((t, GLA_HEADS * GLA_HV), F32),
        scratch_shapes=[
            pltpu.VMEM((GLA_HV, GLA_HK), F32),
            pltpu.VMEM((GLA_CHUNK, GLA_HK), F32),
            pltpu.VMEM((GLA_CHUNK, GLA_HK), F32),
        ],
        compiler_params=_params("parallel", "parallel", "arbitrary"),
        name="gla_recurrence",
    )(proj, proj, proj, glow, w2p, bg)


def _gla_out_kernel(o_ref, g_ref, x_ref, g1_ref, on_ref, w_ref, out_ref):
    parts = []
    for h in range(GLA_HEADS):
        oh = o_ref[:, h * GLA_HV:(h + 1) * GLA_HV]
        ms = jnp.mean(oh * oh, axis=-1, keepdims=True)
        parts.append(oh * lax.rsqrt(ms + EPS) * on_ref[...])
    y = jnp.concatenate(parts, axis=-1) * jax.nn.silu(g_ref[...])
    mix = jnp.dot(y.astype(BF16), w_ref[...], preferred_element_type=F32)
    out_ref[...] = x_ref[...] + g1_ref[0] * mix


def _fox_out_kernel(o_ref, g_ref, x_ref, g1_ref, w_ref, out_ref):
    y = o_ref[...] * jax.nn.sigmoid(g_ref[...])
    mix = jnp.dot(y.astype(BF16), w_ref[...], preferred_element_type=F32)
    out_ref[...] = x_ref[...] + g1_ref[0] * mix


def mixer_out(o, gsrc, gcol, xf, g1, w_bf16, seq, o_norm=None, tm=512):
    t, d = xf.shape
    per_seq = seq // tm
    row = lambda i: (i, 0)
    in_specs = [
        pl.BlockSpec((tm, d), row),
        pl.BlockSpec((tm, d), lambda i: (i, gcol)),
        pl.BlockSpec((tm, d), row),
        pl.BlockSpec((1, 1, d), lambda i: (i // per_seq, 0, 0)),
    ]
    args = [o, gsrc, xf, g1]
    if o_norm is not None:
        in_specs.append(pl.BlockSpec((1, GLA_HV), lambda i: (0, 0)))
        args.append(o_norm.reshape(1, GLA_HV))
        body = _gla_out_kernel
    else:
        body = _fox_out_kernel
    in_specs.append(pl.BlockSpec((d, d), lambda i: (0, 0)))
    args.append(w_bf16)
    return pl.pallas_call(
        body,
        grid=(t // tm,),
        in_specs=in_specs,
        out_specs=pl.BlockSpec((tm, d), row),
        out_shape=jax.ShapeDtypeStruct((t, d), F32),
        compiler_params=_params("parallel"),
        name="mixer_out",
    )(*args)


def _fgate_kernel(f_ref, bf_ref, cum_ref, cumt_ref, carry_ref):
    tc = f_ref.shape[0]

    @pl.when(pl.program_id(1) == 0)
    def _():
        carry_ref[...] = jnp.zeros_like(carry_ref)

    logf = jax.nn.log_sigmoid(f_ref[...] + bf_ref[...])
    tri = (lax.broadcasted_iota(jnp.int32, (tc, tc), 1) <= lax.broadcasted_iota(jnp.int32, (tc, tc), 0)).astype(F32)
    cum = jnp.dot(tri, logf, preferred_element_type=F32, precision=HI) + carry_ref[...]
    cum_ref[...] = cum
    cumt_ref[0] = cum.T[:FOX_HEADS, :]
    carry_ref[...] = cum[tc - 1:tc, :]


def forget_cumsum(f, bf_pad, bsz, seq, tc=256):
    t = f.shape[0]
    ns = seq // tc
    return pl.pallas_call(
        _fgate_kernel,
        grid=(bsz, ns),
        in_specs=[
            pl.BlockSpec((tc, LANES), lambda b, s: (b * ns + s, 0)),
            pl.BlockSpec((1, LANES), lambda b, s: (0, 0)),
        ],
        out_specs=[
            pl.BlockSpec((tc, LANES), lambda b, s: (b * ns + s, 0)),
            pl.BlockSpec((1, FOX_HEADS, tc), lambda b, s: (b, 0, s)),
        ],
        out_shape=[
            jax.ShapeDtypeStruct((t, LANES), F32),
            jax.ShapeDtypeStruct((bsz, FOX_HEADS, seq), F32),
        ],
        scratch_shapes=[pltpu.VMEM((1, LANES), F32)],
        compiler_params=_params("parallel", "arbitrary"),
        name="forget_cumsum",
    )(f, bf_pad)


def _headnorm(xh, gain):
    ms = jnp.mean(xh * xh, axis=-1, keepdims=True)
    return xh * lax.rsqrt(ms + EPS) * gain


def _fox_attn_kernel(q_ref, k_ref, v_ref, cq_ref, ck_ref, qn_ref, kn_ref, o_ref, qs_ref, m_ref, l_ref, acc_ref):
    hp = pl.program_id(1)
    qi = pl.program_id(2)
    ki = pl.program_id(3)
    tq = q_ref.shape[0]
    tk = k_ref.shape[0]

    @pl.when(ki == 0)
    def _():
        for j in range(2):
            qh = q_ref[:, j * FOX_HD:(j + 1) * FOX_HD]
            qs_ref[j] = (_headnorm(qh, qn_ref[...]) * (FOX_HD ** -0.5)).astype(BF16)
        m_ref[...] = jnp.full_like(m_ref, NEG_INF)
        l_ref[...] = jnp.zeros_like(l_ref)
        acc_ref[...] = jnp.zeros_like(acc_ref)

    @pl.when(ki <= qi)
    def _():
        rows = qi * tq + lax.broadcasted_iota(jnp.int32, (tq, tk), 0)
        cols = ki * tk + lax.broadcasted_iota(jnp.int32, (tq, tk), 1)
        causal = cols <= rows
        lane = lax.broadcasted_iota(jnp.int32, (tq, LANES), 1)
        cq_all = cq_ref[...]
        for j in range(2):
            head = 2 * hp + j
            kh = _headnorm(k_ref[:, j * FOX_HD:(j + 1) * FOX_HD], kn_ref[...]).astype(BF16)
            vh = v_ref[:, j * FOX_HD:(j + 1) * FOX_HD].astype(BF16)
            s = lax.dot_general(qs_ref[j], kh, (((1,), (1,)), ((), ())), preferred_element_type=F32)
            cq = jnp.sum(jnp.where(lane == head, cq_all, 0.0), axis=-1, keepdims=True)
            ck = ck_ref[0, pl.ds(head, 1), :]
            s = jnp.where(causal, s + cq - ck, NEG_INF)
            m_prev = m_ref[j]
            m_new = jnp.maximum(m_prev, jnp.max(s, axis=-1, keepdims=True))
            alpha = jnp.exp(m_prev - m_new)
            p = jnp.exp(s - m_new)
            l_ref[j] = alpha * l_ref[j] + jnp.sum(p, axis=-1, keepdims=True)
            acc_ref[j] = alpha * acc_ref[j] + jnp.dot(p.astype(BF16), vh, preferred_element_type=F32)
            m_ref[j] = m_new

    @pl.when(ki == qi)
    def _():
        o_ref[...] = jnp.concatenate([acc_ref[j] / l_ref[j] for j in range(2)], axis=-1)


def fox_attention(qg, kv, cum, cumt, q_norm, k_norm, bsz, seq, tq=512):
    t = qg.shape[0]
    d = FOX_HEADS * FOX_HD
    nq = seq // tq
    npair = FOX_HEADS // 2
    vcol = d // LANES
    return pl.pallas_call(
        _fox_attn_kernel,
        grid=(bsz, npair, nq, nq),
        in_specs=[
            pl.BlockSpec((tq, LANES), lambda b, h, i, j: (b * nq + i, h)),
            pl.BlockSpec((tq, LANES), lambda b, h, i, j: (b * nq + jnp.minimum(i, j), h)),
            pl.BlockSpec((tq, LANES), lambda b, h, i, j: (b * nq + jnp.minimum(i, j), vcol + h)),
            pl.BlockSpec((tq, LANES), lambda b, h, i, j: (b * nq + i, 0)),
            pl.BlockSpec((1, FOX_HEADS, tq), lambda b, h, i, j: (b, 0, jnp.minimum(i, j))),
            pl.BlockSpec((1, FOX_HD), lambda b, h, i, j: (0, 0)),
            pl.BlockSpec((1, FOX_HD), lambda b, h, i, j: (0, 0)),
        ],
        out_specs=pl.BlockSpec((tq, LANES), lambda b, h, i, j: (b * nq + i, h)),
        out_shape=jax.ShapeDtypeStruct((t, d), F32),
        scratch_shapes=[
            pltpu.VMEM((2, tq, FOX_HD), BF16),
            pltpu.VMEM((2, tq, 1), F32),
            pltpu.VMEM((2, tq, 1), F32),
            pltpu.VMEM((2, tq, FOX_HD), F32),
        ],
        compiler_params=_params("parallel", "parallel", "parallel", "arbitrary"),
        name="fox_attention",
    )(qg, kv, kv, cum, cumt, q_norm.reshape(1, FOX_HD), k_norm.reshape(1, FOX_HD))


def _top16_rows(s, want_idx):
    n = s.shape[0]
    row = lax.broadcasted_iota(jnp.int32, s.shape, 0)
    vals, idxs = [], []
    for _ in range(PEER_TOPK):
        m = jnp.max(s, axis=0, keepdims=True)
        idx = jnp.min(jnp.where(s == m, row, n), axis=0, keepdims=True)
        vals.append(m)
        idxs.append(idx)
        s = jnp.where(row == idx, NEG_INF, s)
    vals = jnp.concatenate(vals, axis=0)
    return (vals, jnp.concatenate(idxs, axis=0)) if want_idx else vals


def _scatter_rows(idx, val, n):
    tb = idx.shape[1]
    row = lax.broadcasted_iota(jnp.int32, (n, tb), 0)
    out = jnp.zeros((n, tb), F32)
    for r in range(idx.shape[0]):
        out = jnp.where(row == idx[r:r + 1, :], val[r:r + 1, :], out)
    return out


def _peer_topk_kernel(x_ref, sh_ref, sc_ref, wq_ref, sk_ref, ht_ref, p0_ref, p0s_ref, p1_ref, th_ref):
    h = _adaln_rows(x_ref[...], sh_ref[0], sc_ref[0])
    ht = h.T
    ht_ref[...] = ht.astype(BF16)
    dq = PEER_NKEYS
    thetas = []
    for head in range(PEER_HEADS):
        tops = []
        for half in range(2):
            hp = 2 * head + half
            qt = jnp.dot(wq_ref[hp * dq:(hp + 1) * dq, :], ht, preferred_element_type=F32, precision=HI)
            st = jnp.dot(sk_ref[hp], qt, preferred_element_type=F32, precision=HI)
            tops.append(_top16_rows(st, True))
        (s0, i0), (s1, i1) = tops
        p0 = jnp.exp(s0 - s0[0:1, :])
        p1 = jnp.exp(s1 - s1[0:1, :])
        cand = jnp.concatenate([p0[i:i + 1, :] * p1 for i in range(PEER_TOPK)], axis=0)
        top = _top16_rows(cand, False)
        inv_z = 1.0 / jnp.sum(top, axis=0, keepdims=True)
        thetas.append(top[PEER_TOPK - 1:PEER_TOPK, :])
        t0 = _scatter_rows(i0, p0, PEER_NKEYS)
        p0_ref[head] = t0
        p0s_ref[head] = t0 * inv_z
        p1_ref[head] = _scatter_rows(i1, p1, PEER_NKEYS)
    th_ref[...] = jnp.concatenate(thetas, axis=0)


def peer_topk(xf, shift, scale, wq_t, sub_keys, seq, tb=256):
    t, d = xf.shape
    per_seq = seq // tb
    nh, nk = PEER_HEADS, PEER_NKEYS
    tab = jax.ShapeDtypeStruct((nh, nk, t), F32)
    tab_spec = pl.BlockSpec((nh, nk, tb), lambda i: (0, 0, i))
    return pl.pallas_call(
        _peer_topk_kernel,
        grid=(t // tb,),
        in_specs=[
            pl.BlockSpec((tb, d), lambda i: (i, 0)),
            pl.BlockSpec((1, 1, d), lambda i: (i // per_seq, 0, 0)),
            pl.BlockSpec((1, 1, d), lambda i: (i // per_seq, 0, 0)),
            pl.BlockSpec(wq_t.shape, lambda i: (0, 0)),
            pl.BlockSpec(sub_keys.shape, lambda i: (0, 0, 0)),
        ],
        out_specs=[
            pl.BlockSpec((d, tb), lambda i: (0, i)),
            tab_spec, tab_spec, tab_spec,
            pl.BlockSpec((nh, tb), lambda i: (0, i)),
        ],
        out_shape=[jax.ShapeDtypeStruct((d, t), BF16), tab, tab, tab, jax.ShapeDtypeStruct((nh, t), F32)],
        compiler_params=_params("parallel"),
        name="peer_topk",
    )(xf, shift, scale, wq_t, sub_keys)


def _peer_dense_kernel(ht_ref, u_ref, vt_ref, p0_ref, p0s_ref, p1_ref, th_ref, x_ref, g2_ref, out_ref,
                       at_ref, wt_ref, acc_ref, *, tiles):
    e = pl.program_id(1)
    nk = PEER_NKEYS

    @pl.when(e == 0)
    def _():
        acc_ref[...] = jnp.zeros_like(acc_ref)

    at_ref[...] = jnp.dot(u_ref[...], ht_ref[...], preferred_element_type=F32)

    def tile(i, carry):
        r0 = pl.multiple_of(i * nk, nk)
        key0 = e * tiles + i
        a = at_ref[pl.ds(r0, nk), :]
        act = 0.5 * a * (1.0 + lax.erf(a * (1.0 / math.sqrt(2.0))))
        g = jnp.zeros_like(a)
        for head in range(PEER_HEADS):
            p1 = p1_ref[head]
            sel = p0_ref[head, pl.ds(key0, 1), :] * p1 >= th_ref[pl.ds(head, 1), :]
            g = g + jnp.where(sel, p0s_ref[head, pl.ds(key0, 1), :] * p1, 0.0)
        wt_ref[pl.ds(r0, nk), :] = (g * act).astype(BF16)
        return carry

    lax.fori_loop(0, tiles, tile, 0)
    acc_ref[...] += jnp.dot(vt_ref[...], wt_ref[...], preferred_element_type=F32)

    @pl.when(e == pl.num_programs(1) - 1)
    def _():
        out_ref[...] = x_ref[...] + g2_ref[0] * acc_ref[...].T


def peer_dense(ht, u_bf16, vt_bf16, p0, p0s, p1, theta, xf, g2, seq, tb=256, ec=1024):
    t, d = xf.shape
    ne = u_bf16.shape[0]
    per_seq = seq // tb
    nh, nk = PEER_HEADS, PEER_NKEYS
    tab_spec = pl.BlockSpec((nh, nk, tb), lambda i, e: (0, 0, i))
    return pl.pallas_call(
        functools.partial(_peer_dense_kernel, tiles=ec // nk),
        grid=(t // tb, ne // ec),
        in_specs=[
            pl.BlockSpec((d, tb), lambda i, e: (0, i)),
            pl.BlockSpec((ec, d), lambda i, e: (e, 0)),
            pl.BlockSpec((d, ec), lambda i, e: (0, e)),
            tab_spec, tab_spec, tab_spec,
            pl.BlockSpec((nh, tb), lambda i, e: (0, i)),
            pl.BlockSpec((tb, d), lambda i, e: (i, 0)),
            pl.BlockSpec((1, 1, d), lambda i, e: (i // per_seq, 0, 0)),
        ],
        out_specs=pl.BlockSpec((tb, d), lambda i, e: (i, 0)),
        out_shape=jax.ShapeDtypeStruct((t, d), F32),
        scratch_shapes=[
            pltpu.VMEM((ec, tb), F32),
            pltpu.VMEM((ec, tb), BF16),
            pltpu.VMEM((d, tb), F32),
        ],
        compiler_params=_params("parallel", "arbitrary"),
        name="peer_dense",
    )(ht, u_bf16, vt_bf16, p0, p0s, p1, theta, xf, g2)


def _pad_cols(w, n):
    return jnp.pad(w, ((0, 0), (0, n - w.shape[1])))


def kernel(x, c, mod_w, mod_b, gla_w_in, gla_w_gate2, gla_b_gate, gla_o_norm, gla_w_out, kv_mod_w, kv_mod_b, fox_w_kvf, fox_b_f, fox_k_norm, fox_w_qg, fox_q_norm, fox_w_out, peer_w_q, peer_sub_keys, peer_u, peer_v):
    bsz, seq, d = x.shape
    depth = mod_w.shape[0]
    n_gla = gla_w_in.shape[0]
    xf = x.reshape(bsz * seq, d)
    gla_dk = GLA_HEADS * GLA_HK
    gla_main = 2 * gla_dk + 2 * GLA_HEADS * GLA_HV

    shared = None
    for layer in range(depth):
        if layer == n_gla:
            kv_mod = mod_matmul(c, kv_mod_w, kv_mod_b).reshape(bsz, 2, 1, d)
            kv_sh, kv_sc = kv_mod[:, 0], kv_mod[:, 1]
            kv = adaln_matmul(xf, kv_sh, kv_sc, fox_w_kvf[:, :2 * d].astype(BF16), seq)
            f = adaln_matmul(xf, kv_sh, kv_sc, _pad_cols(fox_w_kvf[:, 2 * d:], LANES), seq)
            bf_pad = jnp.pad(fox_b_f, (0, LANES - FOX_HEADS)).reshape(1, LANES)
            cum, cumt = forget_cumsum(f, bf_pad, bsz, seq)
            shared = (kv, cum, cumt)

        mod = mod_matmul(c, mod_w[layer], mod_b[layer]).reshape(bsz, 6, 1, d)
        sh1, sc1, g1, sh2, sc2, g2 = (mod[:, i] for i in range(6))

        if layer < n_gla:
            w_in = gla_w_in[layer]
            proj = adaln_matmul(xf, sh1, sc1, w_in[:, :gla_main].astype(BF16), seq)
            glow = adaln_matmul(xf, sh1, sc1, _pad_cols(w_in[:, gla_main:], LANES), seq)
            w2p = jnp.pad(gla_w_gate2[layer], ((0, LANES - GLA_GATE_RANK), (0, 0)))
            o = gla_recurrence(proj, glow, w2p, gla_b_gate[layer].reshape(1, gla_dk), bsz, seq)
            xf = mixer_out(o, proj, (2 * gla_dk + GLA_HEADS * GLA_HV) // d, xf, g1,
                           gla_w_out[layer].astype(BF16), seq, o_norm=gla_o_norm[layer])
        else:
            j = layer - n_gla
            kv, cum, cumt = shared
            qg = adaln_matmul(xf, sh1, sc1, fox_w_qg[j].astype(BF16), seq)
            o = fox_attention(qg, kv, cum, cumt, fox_q_norm[j], fox_k_norm, bsz, seq)
            xf = mixer_out(o, qg, 1, xf, g1, fox_w_out[j].astype(BF16), seq)

        sub_keys = peer_sub_keys[layer].reshape(2 * PEER_HEADS, PEER_NKEYS, -1)
        ht, p0, p0s, p1, theta = peer_topk(xf, sh2, sc2, peer_w_q[layer].T, sub_keys, seq)
        xf = peer_dense(ht, peer_u[layer].astype(BF16), peer_v[layer].T.astype(BF16),
                        p0, p0s, p1, theta, xf, g2, seq)
    return xf.reshape(bsz, seq, d)
```

```python
import functools
import math

import jax
import jax.numpy as jnp
from jax import lax
from jax.experimental import pallas as pl
from jax.experimental.pallas import tpu as pltpu

F32 = jnp.float32
BF16 = jnp.bfloat16
HI = lax.Precision.HIGHEST
EPS = 1e-6
NEG_INF = float("-inf")

VMEM_LIMIT_BYTES = 48 * 1024 * 1024
LANES = 128

GLA_HEADS = 4
GLA_HK = 128
GLA_HV = 256
GLA_GATE_RANK = 16
GLA_GATE_NORM = 16.0
GLA_CHUNK = 64

FOX_HD = 64
FOX_HEADS = 16

PEER_HEADS = 8
PEER_NKEYS = 128
PEER_TOPK = 16


def _params(*sem):
    return pltpu.CompilerParams(dimension_semantics=sem, vmem_limit_bytes=VMEM_LIMIT_BYTES)


def _adaln_rows(x, shift, scale):
    ms = jnp.mean(x * x, axis=-1, keepdims=True)
    return x * lax.rsqrt(ms + EPS) * (1.0 + scale) + shift


def _mod_kernel(c_ref, w_ref, b_ref, o_ref):
    o_ref[...] = jnp.dot(c_ref[...], w_ref[...], preferred_element_type=F32, precision=HI) + b_ref[...]


def mod_matmul(c, w, b, tn=512):
    bsz, d = c.shape
    n = w.shape[1]
    return pl.pallas_call(
        _mod_kernel,
        grid=(n // tn,),
        in_specs=[
            pl.BlockSpec((bsz, d), lambda j: (0, 0)),
            pl.BlockSpec((d, tn), lambda j: (0, j)),
            pl.BlockSpec((1, tn), lambda j: (0, j)),
        ],
        out_specs=pl.BlockSpec((bsz, tn), lambda j: (0, j)),
        out_shape=jax.ShapeDtypeStruct((bsz, n), F32),
        compiler_params=_params("parallel"),
        name="mod_matmul",
    )(c, w, b.reshape(1, n))


def _adaln_mm_kernel(x_ref, sh_ref, sc_ref, w_ref, o_ref, h_ref):
    @pl.when(pl.program_id(1) == 0)
    def _():
        h_ref[...] = _adaln_rows(x_ref[...], sh_ref[0], sc_ref[0]).astype(h_ref.dtype)

    if w_ref.dtype == BF16:
        acc = jnp.dot(h_ref[...], w_ref[...], preferred_element_type=F32)
    else:
        acc = jnp.dot(h_ref[...], w_ref[...], preferred_element_type=F32, precision=HI)
    o_ref[...] = acc.astype(o_ref.dtype)


def adaln_matmul(xf, shift, scale, w, seq, tm=512, tn=512):
    t, d = xf.shape
    n = w.shape[1]
    tn = min(tn, n)
    per_seq = seq // tm
    return pl.pallas_call(
        _adaln_mm_kernel,
        grid=(t // tm, n // tn),
        in_specs=[
            pl.BlockSpec((tm, d), lambda i, j: (i, 0)),
            pl.BlockSpec((1, 1, d), lambda i, j: (i // per_seq, 0, 0)),
            pl.BlockSpec((1, 1, d), lambda i, j: (i // per_seq, 0, 0)),
            pl.BlockSpec((d, tn), lambda i, j: (0, j)),
        ],
        out_specs=pl.BlockSpec((tm, tn), lambda i, j: (i, j)),
        out_shape=jax.ShapeDtypeStruct((t, n), F32),
        scratch_shapes=[pltpu.VMEM((tm, d), w.dtype)],
        compiler_params=_params("parallel", "arbitrary"),
        name="adaln_matmul",
    )(xf, shift, scale, w)


def _gla_kernel(q_ref, k_ref, v_ref, gl_ref, w2_ref, bg_ref, o_ref, st_ref, b_scr, k_scr, *, nchunk):
    c = GLA_CHUNK

    @pl.when(pl.program_id(2) == 0)
    def _():
        st_ref[...] = jnp.zeros_like(st_ref)

    row = lax.broadcasted_iota(jnp.int32, (c, GLA_HK), 0)
    lane = lax.broadcasted_iota(jnp.int32, (c, GLA_HK), 1)
    tri = (lax.broadcasted_iota(jnp.int32, (c, c), 1) <= lax.broadcasted_iota(jnp.int32, (c, c), 0)).astype(F32)

    def chunk(ci, carry):
        r0 = pl.multiple_of(ci * c, c)
        q = q_ref[pl.ds(r0, c), :] * (GLA_HK ** -0.5)
        k = k_ref[pl.ds(r0, c), :]
        v = v_ref[pl.ds(r0, c), :]
        pre = jnp.dot(gl_ref[pl.ds(r0, c), :], w2_ref[...], preferred_element_type=F32, precision=HI) + bg_ref[...]
        gk = jax.nn.log_sigmoid(pre) * (1.0 / GLA_GATE_NORM)
        b = jnp.dot(tri, gk, preferred_element_type=F32, precision=HI)
        b_scr[...] = b
        k_scr[...] = k
        st = st_ref[...]
        o_inter = lax.dot_general(q * jnp.exp(b), st, (((1,), (1,)), ((), ())),
                                  preferred_element_type=F32, precision=HI)

        def col(si, attn):
            bs = b_scr[pl.ds(si, 1), :]
            ks = k_scr[pl.ds(si, 1), :]
            rel = jnp.where(row >= si, b - bs, NEG_INF)
            m = jnp.exp(rel) * (q * ks)
            return jnp.where(lane == si, jnp.sum(m, axis=-1, keepdims=True), attn)

        attn = lax.fori_loop(0, c, col, jnp.zeros((c, GLA_HK), F32), unroll=4)
        o = o_inter + jnp.dot(attn[:, :c], v, preferred_element_type=F32, precision=HI)
        o_ref[pl.ds(r0, c), :] = o
        b_last = b[c - 1:c, :]
        kdec = k * jnp.exp(b_last - b)
        st_ref[...] = st * jnp.exp(b_last) + lax.dot_general(
            v, kdec, (((0,), (0,)), ((), ())), preferred_element_type=F32, precision=HI)
        return carry

    lax.fori_loop(0, nchunk, chunk, 0)


def gla_recurrence(proj, glow, w2p, bg, bsz, seq, ts=512):
    t = proj.shape[0]
    ns = seq // ts
    kcol = (GLA_HEADS * GLA_HK) // GLA_HK
    vcol = (2 * GLA_HEADS * GLA_HK) // GLA_HV
    return pl.pallas_call(
        functools.partial(_gla_kernel, nchunk=ts // GLA_CHUNK),
        grid=(bsz, GLA_HEADS, ns),
        in_specs=[
            pl.BlockSpec((ts, GLA_HK), lambda b, h, s: (b * ns + s, h)),
            pl.BlockSpec((ts, GLA_HK), lambda b, h, s: (b * ns + s, kcol + h)),
            pl.BlockSpec((ts, GLA_HV), lambda b, h, s: (b * ns + s, vcol + h)),
            pl.BlockSpec((ts, LANES), lambda b, h, s: (b * ns + s, 0)),
            pl.BlockSpec((LANES, GLA_HK), lambda b, h, s: (0, h)),
            pl.BlockSpec((1, GLA_HK), lambda b, h, s: (0, h)),
        ],
        out_specs=pl.BlockSpec((ts, GLA_HV), lambda b, h, s: (b * ns + s, h)),
        out_shape=jax.ShapeDtypeStruct((t, GLA_HEADS * GLA_HV), F32),
        scratch_shapes=[
            pltpu.VMEM((GLA_HV, GLA_HK), F32),
            pltpu.VMEM((GLA_CHUNK, GLA_HK), F32),
            pltpu.VMEM((GLA_CHUNK, GLA_HK), F32),
        ],
        compiler_params=_params("parallel", "parallel", "arbitrary"),
        name="gla_recurrence",
    )(proj, proj, proj, glow, w2p, bg)


def _gla_out_kernel(o_ref, g_ref, x_ref, g1_ref, on_ref, w_ref, out_ref):
    parts = []
    for h in range(GLA_HEADS):
        oh = o_ref[:, h * GLA_HV:(h + 1) * GLA_HV]
        ms = jnp.mean(oh * oh, axis=-1, keepdims=True)
        parts.append(oh * lax.rsqrt(ms + EPS) * on_ref[...])
    y = jnp.concatenate(parts, axis=-1) * jax.nn.silu(g_ref[...])
    mix = jnp.dot(y.astype(BF16), w_ref[...], preferred_element_type=F32)
    out_ref[...] = x_ref[...] + g1_ref[0] * mix


def _fox_out_kernel(o_ref, g_ref, x_ref, g1_ref, w_ref, out_ref):
    y = o_ref[...] * jax.nn.sigmoid(g_ref[...])
    mix = jnp.dot(y.astype(BF16), w_ref[...], preferred_element_type=F32)
    out_ref[...] = x_ref[...] + g1_ref[0] * mix


def mixer_out(o, gsrc, gcol, xf, g1, w_bf16, seq, o_norm=None, tm=512):
    t, d = xf.shape
    per_seq = seq // tm
    row = lambda i: (i, 0)
    in_specs = [
        pl.BlockSpec((tm, d), row),
        pl.BlockSpec((tm, d), lambda i: (i, gcol)),
        pl.BlockSpec((tm, d), row),
        pl.BlockSpec((1, 1, d), lambda i: (i // per_seq, 0, 0)),
    ]
    args = [o, gsrc, xf, g1]
    if o_norm is not None:
        in_specs.append(pl.BlockSpec((1, GLA_HV), lambda i: (0, 0)))
        args.append(o_norm.reshape(1, GLA_HV))
        body = _gla_out_kernel
    else:
        body = _fox_out_kernel
    in_specs.append(pl.BlockSpec((d, d), lambda i: (0, 0)))
    args.append(w_bf16)
    return pl.pallas_call(
        body,
        grid=(t // tm,),
        in_specs=in_specs,
        out_specs=pl.BlockSpec((tm, d), row),
        out_shape=jax.ShapeDtypeStruct((t, d), F32),
        compiler_params=_params("parallel"),
        name="mixer_out",
    )(*args)


def _fgate_kernel(f_ref, bf_ref, cumt_ref, carry_ref):
    tc = f_ref.shape[0]

    @pl.when(pl.program_id(1) == 0)
    def _():
        carry_ref[...] = jnp.zeros_like(carry_ref)

    logf = jax.nn.log_sigmoid(f_ref[...] + bf_ref[...])
    tri = (lax.broadcasted_iota(jnp.int32, (tc, tc), 1) <= lax.broadcasted_iota(jnp.int32, (tc, tc), 0)).astype(F32)
    cum = jnp.dot(tri, logf, preferred_element_type=F32, precision=HI) + carry_ref[...]
    cumt_ref[0] = cum.T[:FOX_HEADS, :]
    carry_ref[...] = cum[tc - 1:tc, :]


def forget_cumsum(f, bf_pad, bsz, seq, tc=256):
    t = f.shape[0]
    ns = seq // tc
    return pl.pallas_call(
        _fgate_kernel,
        grid=(bsz, ns),
        in_specs=[
            pl.BlockSpec((tc, LANES), lambda b, s: (b * ns + s, 0)),
            pl.BlockSpec((1, LANES), lambda b, s: (0, 0)),
        ],
        out_specs=pl.BlockSpec((1, FOX_HEADS, tc), lambda b, s: (b, 0, s)),
        out_shape=jax.ShapeDtypeStruct((bsz, FOX_HEADS, seq), F32),
        scratch_shapes=[pltpu.VMEM((1, LANES), F32)],
        compiler_params=_params("parallel", "arbitrary"),
        name="forget_cumsum",
    )(f, bf_pad)


def _pair_headnorm(x2, bd, gain2):
    ms = jnp.dot(x2 * x2, bd, preferred_element_type=F32, precision=HI)
    return x2 * lax.rsqrt(ms + EPS) * gain2


def _kv_prep_kernel(kv_ref, bd_ref, kn_ref, ko_ref, vo_ref):
    d = ko_ref.shape[1]
    for cb in range(d // LANES):
        cols = slice(cb * LANES, (cb + 1) * LANES)
        ko_ref[:, cols] = _pair_headnorm(kv_ref[:, cols], bd_ref[...], kn_ref[...]).astype(BF16)
    vo_ref[...] = kv_ref[:, d:].astype(BF16)


def fox_kv_prep(kv, bd, k_gain2, tm=512):
    t, d2 = kv.shape
    d = d2 // 2
    out = jax.ShapeDtypeStruct((t, d), BF16)
    return pl.pallas_call(
        _kv_prep_kernel,
        grid=(t // tm,),
        in_specs=[
            pl.BlockSpec((tm, d2), lambda i: (i, 0)),
            pl.BlockSpec((LANES, LANES), lambda i: (0, 0)),
            pl.BlockSpec((1, LANES), lambda i: (0, 0)),
        ],
        out_specs=[pl.BlockSpec((tm, d), lambda i: (i, 0))] * 2,
        out_shape=[out, out],
        compiler_params=_params("parallel"),
        name="fox_kv_prep",
    )(kv, bd, k_gain2)


def _fox_attn_kernel(q_ref, k_ref, v_ref, ck_ref, bd_ref, qn_ref, o_ref, qs_ref, m_ref, l_ref, acc_ref):
    hp = pl.program_id(1)
    qi = pl.program_id(2)
    ki = pl.program_id(3)
    tq = q_ref.shape[0]
    tk = k_ref.shape[0]

    @pl.when(ki == 0)
    def _():
        qn = _pair_headnorm(q_ref[...], bd_ref[...], qn_ref[...]) * (FOX_HD ** -0.5)
        lane = lax.broadcasted_iota(jnp.int32, qn.shape, 1)
        qs_ref[0] = jnp.where(lane < FOX_HD, qn, 0.0).astype(BF16)
        qs_ref[1] = jnp.where(lane >= FOX_HD, qn, 0.0).astype(BF16)
        m_ref[...] = jnp.full_like(m_ref, NEG_INF)
        l_ref[...] = jnp.zeros_like(l_ref)
        acc_ref[...] = jnp.zeros_like(acc_ref)

    def update(diagonal):
        kb = k_ref[...]
        vb = v_ref[...]
        if diagonal:
            causal = (lax.broadcasted_iota(jnp.int32, (tq, tk), 1) <= lax.broadcasted_iota(jnp.int32, (tq, tk), 0))
        for j in range(2):
            s = lax.dot_general(qs_ref[j], kb, (((1,), (1,)), ((), ())), preferred_element_type=F32)
            s = s - ck_ref[0, pl.ds(2 * hp + j, 1), :]
            if diagonal:
                s = jnp.where(causal, s, NEG_INF)
            m_prev = m_ref[j]
            m_new = jnp.maximum(m_prev, jnp.max(s, axis=-1, keepdims=True))
            alpha = jnp.exp(m_prev - m_new)
            p = jnp.exp(s - m_new)
            l_ref[j] = alpha * l_ref[j] + jnp.sum(p, axis=-1, keepdims=True)
            acc_ref[j] = alpha * acc_ref[j] + jnp.dot(p.astype(BF16), vb, preferred_element_type=F32)
            m_ref[j] = m_new

    @pl.when(ki < qi)
    def _():
        update(False)

    @pl.when(ki == qi)
    def _():
        update(True)
        lane = lax.broadcasted_iota(jnp.int32, (tq, LANES), 1)
        o_ref[...] = jnp.where(lane < FOX_HD, acc_ref[0] / l_ref[0], acc_ref[1] / l_ref[1])


def fox_attention(qg, kn, vb, cumt, bd, q_gain2, bsz, seq, tq=512):
    t = qg.shape[0]
    d = FOX_HEADS * FOX_HD
    nq = seq // tq
    npair = FOX_HEADS // 2
    return pl.pallas_call(
        _fox_attn_kernel,
        grid=(bsz, npair, nq, nq),
        in_specs=[
            pl.BlockSpec((tq, LANES), lambda b, h, i, j: (b * nq + i, h)),
            pl.BlockSpec((tq, LANES), lambda b, h, i, j: (b * nq + jnp.minimum(i, j), h)),
            pl.BlockSpec((tq, LANES), lambda b, h, i, j: (b * nq + jnp.minimum(i, j), h)),
            pl.BlockSpec((1, FOX_HEADS, tq), lambda b, h, i, j: (b, 0, jnp.minimum(i, j))),
            pl.BlockSpec((LANES, LANES), lambda b, h, i, j: (0, 0)),
            pl.BlockSpec((1, LANES), lambda b, h, i, j: (0, 0)),
        ],
        out_specs=pl.BlockSpec((tq, LANES), lambda b, h, i, j: (b * nq + i, h)),
        out_shape=jax.ShapeDtypeStruct((t, d), F32),
        scratch_shapes=[
            pltpu.VMEM((2, tq, LANES), BF16),
            pltpu.VMEM((2, tq, 1), F32),
            pltpu.VMEM((2, tq, 1), F32),
            pltpu.VMEM((2, tq, LANES), F32),
        ],
        compiler_params=_params("parallel", "parallel", "parallel", "arbitrary"),
        name="fox_attention",
    )(qg, kn, vb, cumt, bd, q_gain2)


SUBLANES = 8


def _merge_desc(v):
    n = len(v)
    if n == 1:
        return v
    half = n // 2
    hi = [jnp.maximum(v[i], v[i + half]) for i in range(half)]
    lo = [jnp.minimum(v[i], v[i + half]) for i in range(half)]
    return _merge_desc(hi) + _merge_desc(lo)


def _sort_desc(v):
    n = len(v)
    if n == 1:
        return v
    return _merge_desc(_sort_desc(v[:n // 2]) + _sort_desc(v[n // 2:])[::-1])


def _top16_values(groups):
    lists = _sort_desc(groups)
    shift = SUBLANES // 2
    while shift >= 1:
        partner = [pltpu.roll(a, shift, 0) for a in lists]
        if len(lists) < PEER_TOPK:
            lists = _merge_desc(lists + partner[::-1])
        else:
            n = len(lists)
            lists = _merge_desc([jnp.maximum(lists[i], partner[n - 1 - i]) for i in range(n)])
        shift //= 2
    return lists


_CAND_PAIRS = [(i, j) for i in range(PEER_TOPK) for j in range(PEER_TOPK) if (i + 1) * (j + 1) <= PEER_TOPK]


def _peer_topk_kernel(x_ref, sh_ref, sc_ref, wq_ref, sk_ref, ht_ref, p0_ref, p0s_ref, p1_ref, th_ref, qt_ref):
    h = _adaln_rows(x_ref[...], sh_ref[0], sc_ref[0])
    ht = h.T
    ht_ref[...] = ht.astype(BF16)
    dq = PEER_NKEYS
    ngroups = PEER_NKEYS // SUBLANES
    tb = ht.shape[1]
    sub = lax.broadcasted_iota(jnp.int32, (SUBLANES, tb), 0)
    thetas = []
    qt_ref[...] = jnp.dot(wq_ref[...], ht, preferred_element_type=F32, precision=HI)
    for head in range(PEER_HEADS):
        groups, tops = [], []
        for half in range(2):
            hp = 2 * head + half
            st = jnp.dot(sk_ref[hp], qt_ref[hp * dq:(hp + 1) * dq, :],
                         preferred_element_type=F32, precision=HI)
            g = [st[SUBLANES * k:SUBLANES * (k + 1), :] for k in range(ngroups)]
            m = functools.reduce(jnp.maximum, g)
            for shift in (4, 2, 1):
                m = jnp.maximum(m, pltpu.roll(m, shift, 0))
            g = [jnp.exp(v - m) for v in g]
            groups.append(g)
            tops.append(_top16_values(g))
        p0, p1 = tops
        packed = []
        for k in range(0, len(_CAND_PAIRS), SUBLANES):
            acc = jnp.zeros((SUBLANES, tb), F32)
            for s_, (i, j) in enumerate(_CAND_PAIRS[k:k + SUBLANES]):
                acc = jnp.where(sub == s_, p0[i] * p1[j], acc)
            packed.append(acc)
        while len(packed) & (len(packed) - 1):
            packed.append(jnp.zeros((SUBLANES, tb), F32))
        top = _top16_values(packed)
        z = top[0]
        for v in top[1:]:
            z = z + v
        inv_z = 1.0 / z
        thetas.append(top[PEER_TOPK - 1][0:1, :])
        for half, (ref_list) in enumerate(((p0_ref, p0s_ref), (p1_ref,))):
            t16 = tops[half][PEER_TOPK - 1]
            for k in range(ngroups):
                g = groups[half][k]
                tab = jnp.where(g >= t16, g, 0.0)
                rows = slice(SUBLANES * k, SUBLANES * (k + 1))
                ref_list[0][head, rows, :] = tab
                if half == 0:
                    ref_list[1][head, rows, :] = tab * inv_z
    th_ref[...] = jnp.concatenate(thetas, axis=0)


def peer_topk(xf, shift, scale, wq_t, sub_keys, seq, tb=256):
    t, d = xf.shape
    per_seq = seq // tb
    nh, nk = PEER_HEADS, PEER_NKEYS
    tab = jax.ShapeDtypeStruct((nh, nk, t), F32)
    tab_spec = pl.BlockSpec((nh, nk, tb), lambda i: (0, 0, i))
    return pl.pallas_call(
        _peer_topk_kernel,
        grid=(t // tb,),
        in_specs=[
            pl.BlockSpec((tb, d), lambda i: (i, 0)),
            pl.BlockSpec((1, 1, d), lambda i: (i // per_seq, 0, 0)),
            pl.BlockSpec((1, 1, d), lambda i: (i // per_seq, 0, 0)),
            pl.BlockSpec(wq_t.shape, lambda i: (0, 0)),
            pl.BlockSpec(sub_keys.shape, lambda i: (0, 0, 0)),
        ],
        out_specs=[
            pl.BlockSpec((d, tb), lambda i: (0, i)),
            tab_spec, tab_spec, tab_spec,
            pl.BlockSpec((nh, tb), lambda i: (0, i)),
        ],
        out_shape=[jax.ShapeDtypeStruct((d, t), BF16), tab, tab, tab, jax.ShapeDtypeStruct((nh, t), F32)],
        scratch_shapes=[pltpu.VMEM((wq_t.shape[0], tb), F32)],
        compiler_params=_params("parallel"),
        name="peer_topk",
    )(xf, shift, scale, wq_t, sub_keys)


def _peer_dense_kernel(ht_ref, u_ref, vt_ref, p0_ref, p0s_ref, p1_ref, th_ref, x_ref, g2_ref, out_ref,
                       at_ref, wt_ref, acc_ref, *, tiles):
    e = pl.program_id(1)
    nk = PEER_NKEYS
    slot = e % 2
    prev = 1 - slot

    @pl.when(e == 0)
    def _():
        acc_ref[...] = jnp.zeros_like(acc_ref)
        at_ref[1] = jnp.zeros(at_ref.shape[1:], F32)

    a_new = jnp.dot(u_ref[...], ht_ref[...], preferred_element_type=F32)

    key_base = (e - 1) * tiles
    for i in range(tiles):
        rows = slice(i * nk, (i + 1) * nk)
        key0 = jnp.maximum(key_base + i, 0)
        a = at_ref[prev, rows, :]
        act = 0.5 * a * (1.0 + lax.erf(a * (1.0 / math.sqrt(2.0))))
        g = jnp.zeros_like(a)
        for head in range(PEER_HEADS):
            p1 = p1_ref[head]
            sel = p0_ref[head, pl.ds(key0, 1), :] * p1 >= th_ref[pl.ds(head, 1), :]
            g = g + jnp.where(sel, p0s_ref[head, pl.ds(key0, 1), :] * p1, 0.0)
        wt_ref[rows, :] = (g * act).astype(BF16)

    acc_ref[...] += jnp.dot(vt_ref[...], wt_ref[...], preferred_element_type=F32)
    at_ref[slot] = a_new

    @pl.when(e == pl.num_programs(1) - 1)
    def _():
        out_ref[...] = x_ref[...] + g2_ref[0] * acc_ref[...].T


def peer_dense(ht, u_bf16, vt_bf16, p0, p0s, p1, theta, xf, g2, seq, tb=256, ec=1024):
    t, d = xf.shape
    ne = u_bf16.shape[0]
    nchunk = ne // ec
    per_seq = seq // tb
    nh, nk = PEER_HEADS, PEER_NKEYS
    tab_spec = pl.BlockSpec((nh, nk, tb), lambda i, e: (0, 0, i))
    return pl.pallas_call(
        functools.partial(_peer_dense_kernel, tiles=ec // nk),
        grid=(t // tb, nchunk + 1),
        in_specs=[
            pl.BlockSpec((d, tb), lambda i, e: (0, i)),
            pl.BlockSpec((ec, d), lambda i, e: (jnp.minimum(e, nchunk - 1), 0)),
            pl.BlockSpec((d, ec), lambda i, e: (0, jnp.maximum(e - 1, 0))),
            tab_spec, tab_spec, tab_spec,
            pl.BlockSpec((nh, tb), lambda i, e: (0, i)),
            pl.BlockSpec((tb, d), lambda i, e: (i, 0)),
            pl.BlockSpec((1, 1, d), lambda i, e: (i // per_seq, 0, 0)),
        ],
        out_specs=pl.BlockSpec((tb, d), lambda i, e: (i, 0)),
        out_shape=jax.ShapeDtypeStruct((t, d), F32),
        scratch_shapes=[
            pltpu.VMEM((2, ec, tb), F32),
            pltpu.VMEM((ec, tb), BF16),
            pltpu.VMEM((d, tb), F32),
        ],
        compiler_params=_params("parallel", "arbitrary"),
        name="peer_dense",
    )(ht, u_bf16, vt_bf16, p0, p0s, p1, theta, xf, g2)


def _pad_cols(w, n):
    return jnp.pad(w, ((0, 0), (0, n - w.shape[1])))


def kernel(x, c, mod_w, mod_b, gla_w_in, gla_w_gate2, gla_b_gate, gla_o_norm, gla_w_out, kv_mod_w, kv_mod_b, fox_w_kvf, fox_b_f, fox_k_norm, fox_w_qg, fox_q_norm, fox_w_out, peer_w_q, peer_sub_keys, peer_u, peer_v):
    bsz, seq, d = x.shape
    depth = mod_w.shape[0]
    n_gla = gla_w_in.shape[0]
    xf = x.reshape(bsz * seq, d)
    gla_dk = GLA_HEADS * GLA_HK
    gla_main = 2 * gla_dk + 2 * GLA_HEADS * GLA_HV

    shared = None
    for layer in range(depth):
        if layer == n_gla:
            kv_mod = mod_matmul(c, kv_mod_w, kv_mod_b).reshape(bsz, 2, 1, d)
            kv_sh, kv_sc = kv_mod[:, 0], kv_mod[:, 1]
            kv = adaln_matmul(xf, kv_sh, kv_sc, fox_w_kvf[:, :2 * d].astype(BF16), seq)
            f = adaln_matmul(xf, kv_sh, kv_sc, _pad_cols(fox_w_kvf[:, 2 * d:], LANES), seq)
            bf_pad = jnp.pad(fox_b_f, (0, LANES - FOX_HEADS)).reshape(1, LANES)
            cumt = forget_cumsum(f, bf_pad, bsz, seq)
            head_avg = jnp.kron(jnp.eye(LANES // FOX_HD, dtype=F32), jnp.full((FOX_HD, FOX_HD), 1.0 / FOX_HD, F32))
            kn, vb = fox_kv_prep(kv, head_avg, jnp.tile(fox_k_norm, LANES // FOX_HD).reshape(1, LANES))
            shared = (kn, vb, cumt, head_avg)

        mod = mod_matmul(c, mod_w[layer], mod_b[layer]).reshape(bsz, 6, 1, d)
        sh1, sc1, g1, sh2, sc2, g2 = (mod[:, i] for i in range(6))

        if layer < n_gla:
            w_in = gla_w_in[layer]
            proj = adaln_matmul(xf, sh1, sc1, w_in[:, :gla_main].astype(BF16), seq)
            glow = adaln_matmul(xf, sh1, sc1, _pad_cols(w_in[:, gla_main:], LANES), seq)
            w2p = jnp.pad(gla_w_gate2[layer], ((0, LANES - GLA_GATE_RANK), (0, 0)))
            o = gla_recurrence(proj, glow, w2p, gla_b_gate[layer].reshape(1, gla_dk), bsz, seq)
            xf = mixer_out(o, proj, (2 * gla_dk + GLA_HEADS * GLA_HV) // d, xf, g1,
                           gla_w_out[layer].astype(BF16), seq, o_norm=gla_o_norm[layer])
        else:
            j = layer - n_gla
            kn, vb, cumt, head_avg = shared
            qg = adaln_matmul(xf, sh1, sc1, fox_w_qg[j].astype(BF16), seq)
            o = fox_attention(qg, kn, vb, cumt, head_avg,
                              jnp.tile(fox_q_norm[j], LANES // FOX_HD).reshape(1, LANES), bsz, seq)
            xf = mixer_out(o, qg, 1, xf, g1, fox_w_out[j].astype(BF16), seq)

        sub_keys = peer_sub_keys[layer].reshape(2 * PEER_HEADS, PEER_NKEYS, -1)
        ht, p0, p0s, p1, theta = peer_topk(xf, sh2, sc2, peer_w_q[layer].T, sub_keys, seq)
        xf = peer_dense(ht, peer_u[layer].astype(BF16), peer_v[layer].T.astype(BF16),
                        p0, p0s, p1, theta, xf, g2, seq)
    return xf.reshape(bsz, seq, d)
```

```python
import functools
import math

import jax
import jax.numpy as jnp
from jax import lax
from jax.experimental import pallas as pl
from jax.experimental.pallas import tpu as pltpu

F32 = jnp.float32
BF16 = jnp.bfloat16
HI = lax.Precision.HIGHEST
EPS = 1e-6
NEG_INF = float("-inf")

VMEM_LIMIT_BYTES = 48 * 1024 * 1024
LANES = 128

GLA_HEADS = 4
GLA_HK = 128
GLA_HV = 256
GLA_GATE_RANK = 16
GLA_GATE_NORM = 16.0
GLA_CHUNK = 64

FOX_HD = 64
FOX_HEADS = 16

PEER_HEADS = 8
PEER_NKEYS = 128
PEER_TOPK = 16


def _params(*sem):
    return pltpu.CompilerParams(dimension_semantics=sem, vmem_limit_bytes=VMEM_LIMIT_BYTES)


def _adaln_rows(x, shift, scale):
    ms = jnp.mean(x * x, axis=-1, keepdims=True)
    return x * lax.rsqrt(ms + EPS) * (1.0 + scale) + shift


def _mod_kernel(c_ref, w_ref, b_ref, o_ref):
    o_ref[...] = jnp.dot(c_ref[...], w_ref[...], preferred_element_type=F32, precision=HI) + b_ref[...]


def mod_matmul(c, w, b, tn=512):
    bsz, d = c.shape
    n = w.shape[1]
    return pl.pallas_call(
        _mod_kernel,
        grid=(n // tn,),
        in_specs=[
            pl.BlockSpec((bsz, d), lambda j: (0, 0)),
            pl.BlockSpec((d, tn), lambda j: (0, j)),
            pl.BlockSpec((1, tn), lambda j: (0, j)),
        ],
        out_specs=pl.BlockSpec((bsz, tn), lambda j: (0, j)),
        out_shape=jax.ShapeDtypeStruct((bsz, n), F32),
        compiler_params=_params("parallel"),
        name="mod_matmul",
    )(c, w, b.reshape(1, n))


def _adaln_mm_kernel(x_ref, sh_ref, sc_ref, w_ref, o_ref, h_ref):
    @pl.when(pl.program_id(1) == 0)
    def _():
        h_ref[...] = _adaln_rows(x_ref[...], sh_ref[0], sc_ref[0]).astype(h_ref.dtype)

    if w_ref.dtype == BF16:
        acc = jnp.dot(h_ref[...], w_ref[...], preferred_element_type=F32)
    else:
        acc = jnp.dot(h_ref[...], w_ref[...], preferred_element_type=F32, precision=HI)
    o_ref[...] = acc.astype(o_ref.dtype)


def adaln_matmul(xf, shift, scale, w, seq, tm=512, tn=512):
    t, d = xf.shape
    n = w.shape[1]
    tn = min(tn, n)
    per_seq = seq // tm
    return pl.pallas_call(
        _adaln_mm_kernel,
        grid=(t // tm, n // tn),
        in_specs=[
            pl.BlockSpec((tm, d), lambda i, j: (i, 0)),
            pl.BlockSpec((1, 1, d), lambda i, j: (i // per_seq, 0, 0)),
            pl.BlockSpec((1, 1, d), lambda i, j: (i // per_seq, 0, 0)),
            pl.BlockSpec((d, tn), lambda i, j: (0, j)),
        ],
        out_specs=pl.BlockSpec((tm, tn), lambda i, j: (i, j)),
        out_shape=jax.ShapeDtypeStruct((t, n), F32),
        scratch_shapes=[pltpu.VMEM((tm, d), w.dtype)],
        compiler_params=_params("parallel", "arbitrary"),
        name="adaln_matmul",
    )(xf, shift, scale, w)


def _gla_kernel(q_ref, k_ref, v_ref, gl_ref, w2_ref, bg_ref, o_ref, st_ref, b_scr, k_scr, *, nchunk):
    c = GLA_CHUNK

    @pl.when(pl.program_id(2) == 0)
    def _():
        st_ref[...] = jnp.zeros_like(st_ref)

    row = lax.broadcasted_iota(jnp.int32, (c, GLA_HK), 0)
    lane = lax.broadcasted_iota(jnp.int32, (c, GLA_HK), 1)
    tri = (lax.broadcasted_iota(jnp.int32, (c, c), 1) <= lax.broadcasted_iota(jnp.int32, (c, c), 0)).astype(F32)

    def chunk(ci, carry):
        r0 = pl.multiple_of(ci * c, c)
        q = q_ref[pl.ds(r0, c), :] * (GLA_HK ** -0.5)
        k = k_ref[pl.ds(r0, c), :]
        v = v_ref[pl.ds(r0, c), :]
        pre = jnp.dot(gl_ref[pl.ds(r0, c), :], w2_ref[...], preferred_element_type=F32, precision=HI) + bg_ref[...]
        gk = jax.nn.log_sigmoid(pre) * (1.0 / GLA_GATE_NORM)
        b = jnp.dot(tri, gk, preferred_element_type=F32, precision=HI)
        b_scr[...] = b
        k_scr[...] = k
        st = st_ref[...]
        o_inter = lax.dot_general(q * jnp.exp(b), st, (((1,), (1,)), ((), ())),
                                  preferred_element_type=F32, precision=HI)

        def col(si, attn):
            bs = b_scr[pl.ds(si, 1), :]
            ks = k_scr[pl.ds(si, 1), :]
            rel = jnp.where(row >= si, b - bs, NEG_INF)
            m = jnp.exp(rel) * (q * ks)
            return jnp.where(lane == si, jnp.sum(m, axis=-1, keepdims=True), attn)

        attn = lax.fori_loop(0, c, col, jnp.zeros((c, GLA_HK), F32), unroll=4)
        o = o_inter + jnp.dot(attn[:, :c], v, preferred_element_type=F32, precision=HI)
        o_ref[pl.ds(r0, c), :] = o
        b_last = b[c - 1:c, :]
        kdec = k * jnp.exp(b_last - b)
        st_ref[...] = st * jnp.exp(b_last) + lax.dot_general(
            v, kdec, (((0,), (0,)), ((), ())), preferred_element_type=F32, precision=HI)
        return carry

    lax.fori_loop(0, nchunk, chunk, 0)


def gla_recurrence(proj, glow, w2p, bg, bsz, seq, ts=512):
    t = proj.shape[0]
    ns = seq // ts
    kcol = (GLA_HEADS * GLA_HK) // GLA_HK
    vcol = (2 * GLA_HEADS * GLA_HK) // GLA_HV
    return pl.pallas_call(
        functools.partial(_gla_kernel, nchunk=ts // GLA_CHUNK),
        grid=(bsz, GLA_HEADS, ns),
        in_specs=[
            pl.BlockSpec((ts, GLA_HK), lambda b, h, s: (b * ns + s, h)),
            pl.BlockSpec((ts, GLA_HK), lambda b, h, s: (b * ns + s, kcol + h)),
            pl.BlockSpec((ts, GLA_HV), lambda b, h, s: (b * ns + s, vcol + h)),
            pl.BlockSpec((ts, LANES), lambda b, h, s: (b * ns + s, 0)),
            pl.BlockSpec((LANES, GLA_HK), lambda b, h, s: (0, h)),
            pl.BlockSpec((1, GLA_HK), lambda b, h, s: (0, h)),
        ],
        out_specs=pl.BlockSpec((ts, GLA_HV), lambda b, h, s: (b * ns + s, h)),
        out_shape=jax.ShapeDtypeStruct((t, GLA_HEADS * GLA_HV), F32),
        scratch_shapes=[
            pltpu.VMEM((GLA_HV, GLA_HK), F32),
            pltpu.VMEM((GLA_CHUNK, GLA_HK), F32),
            pltpu.VMEM((GLA_CHUNK, GLA_HK), F32),
        ],
        compiler_params=_params("parallel", "parallel", "arbitrary"),
        name="gla_recurrence",
    )(proj, proj, proj, glow, w2p, bg)


def _gla_out_kernel(o_ref, g_ref, x_ref, g1_ref, on_ref, w_ref, out_ref):
    parts = []
    for h in range(GLA_HEADS):
        oh = o_ref[:, h * GLA_HV:(h + 1) * GLA_HV]
        ms = jnp.mean(oh * oh, axis=-1, keepdims=True)
        parts.append(oh * lax.rsqrt(ms + EPS) * on_ref[...])
    y = jnp.concatenate(parts, axis=-1) * jax.nn.silu(g_ref[...])
    mix = jnp.dot(y.astype(BF16), w_ref[...], preferred_element_type=F32)
    out_ref[...] = x_ref[...] + g1_ref[0] * mix


def _fox_out_kernel(o_ref, g_ref, x_ref, g1_ref, w_ref, out_ref):
    y = o_ref[...] * jax.nn.sigmoid(g_ref[...])
    mix = jnp.dot(y.astype(BF16), w_ref[...], preferred_element_type=F32)
    out_ref[...] = x_ref[...] + g1_ref[0] * mix


def mixer_out(o, gsrc, gcol, xf, g1, w_bf16, seq, o_norm=None, tm=512):
    t, d = xf.shape
    per_seq = seq // tm
    row = lambda i: (i, 0)
    in_specs = [
        pl.BlockSpec((tm, d), row),
        pl.BlockSpec((tm, d), lambda i: (i, gcol)),
        pl.BlockSpec((tm, d), row),
        pl.BlockSpec((1, 1, d), lambda i: (i // per_seq, 0, 0)),
    ]
    args = [o, gsrc, xf, g1]
    if o_norm is not None:
        in_specs.append(pl.BlockSpec((1, GLA_HV), lambda i: (0, 0)))
        args.append(o_norm.reshape(1, GLA_HV))
        body = _gla_out_kernel
    else:
        body = _fox_out_kernel
    in_specs.append(pl.BlockSpec((d, d), lambda i: (0, 0)))
    args.append(w_bf16)
    return pl.pallas_call(
        body,
        grid=(t // tm,),
        in_specs=in_specs,
        out_specs=pl.BlockSpec((tm, d), row),
        out_shape=jax.ShapeDtypeStruct((t, d), F32),
        compiler_params=_params("parallel"),
        name="mixer_out",
    )(*args)


FOX_BIAS_PIECES = 3
FOX_BIAS_LANES = 2 * FOX_BIAS_PIECES


def _fgate_kernel(f_ref, bf_ref, kb_ref, carry_ref):
    tc = f_ref.shape[0]

    @pl.when(pl.program_id(1) == 0)
    def _():
        carry_ref[...] = jnp.zeros_like(carry_ref)

    logf = jax.nn.log_sigmoid(f_ref[...] + bf_ref[...])
    tri = (lax.broadcasted_iota(jnp.int32, (tc, tc), 1) <= lax.broadcasted_iota(jnp.int32, (tc, tc), 0)).astype(F32)
    cum = jnp.dot(tri, logf, preferred_element_type=F32, precision=HI) + carry_ref[...]
    carry_ref[...] = cum[tc - 1:tc, :]
    hi = cum.astype(BF16).astype(F32)
    mid = (cum - hi).astype(BF16).astype(F32)
    lo = cum - hi - mid
    pos = lax.broadcasted_iota(jnp.int32, cum.shape, 1) % LANES
    piece = pos % FOX_BIAS_PIECES
    out = jnp.where(piece == 0, hi, jnp.where(piece == 1, mid, lo))
    kb_ref[...] = jnp.where(pos < FOX_BIAS_LANES, out, 0.0).astype(BF16)


def forget_cumsum(f_ext, bf_ext, bsz, seq, tc=256):
    t, n = f_ext.shape
    ns = seq // tc
    return pl.pallas_call(
        _fgate_kernel,
        grid=(bsz, ns),
        in_specs=[
            pl.BlockSpec((tc, n), lambda b, s: (b * ns + s, 0)),
            pl.BlockSpec((1, n), lambda b, s: (0, 0)),
        ],
        out_specs=pl.BlockSpec((tc, n), lambda b, s: (b * ns + s, 0)),
        out_shape=jax.ShapeDtypeStruct((t, n), BF16),
        scratch_shapes=[pltpu.VMEM((1, n), F32)],
        compiler_params=_params("parallel", "arbitrary"),
        name="forget_cumsum",
    )(f_ext, bf_ext)


def _pair_headnorm(x2, bd, gain2):
    ms = jnp.dot(x2 * x2, bd, preferred_element_type=F32, precision=HI)
    return x2 * lax.rsqrt(ms + EPS) * gain2


def _kv_prep_kernel(kv_ref, bd_ref, kn_ref, ko_ref, vt_ref):
    d = ko_ref.shape[1]
    for cb in range(d // LANES):
        cols = slice(cb * LANES, (cb + 1) * LANES)
        ko_ref[:, cols] = _pair_headnorm(kv_ref[:, cols], bd_ref[...], kn_ref[...]).astype(BF16)
        vt_ref[cols, :] = kv_ref[:, d + cb * LANES:d + (cb + 1) * LANES].T.astype(BF16)


def fox_kv_prep(kv, bd, k_gain2, tm=512):
    t, d2 = kv.shape
    d = d2 // 2
    return pl.pallas_call(
        _kv_prep_kernel,
        grid=(t // tm,),
        in_specs=[
            pl.BlockSpec((tm, d2), lambda i: (i, 0)),
            pl.BlockSpec((LANES, LANES), lambda i: (0, 0)),
            pl.BlockSpec((1, LANES), lambda i: (0, 0)),
        ],
        out_specs=[pl.BlockSpec((tm, d), lambda i: (i, 0)), pl.BlockSpec((d, tm), lambda i: (0, i))],
        out_shape=[jax.ShapeDtypeStruct((t, d), BF16), jax.ShapeDtypeStruct((d, t), BF16)],
        compiler_params=_params("parallel"),
        name="fox_kv_prep",
    )(kv, bd, k_gain2)


def _fox_attn_kernel(q_ref, k_ref, kb_ref, vt_ref, bd_ref, qn_ref, o_ref, qs_ref, m_ref, l_ref, acc_ref):
    qi = pl.program_id(2)
    ki = pl.program_id(3)
    tq = q_ref.shape[0]
    tk = k_ref.shape[0]

    @pl.when(ki == 0)
    def _():
        qt = (_pair_headnorm(q_ref[...], bd_ref[...], qn_ref[...]) * (FOX_HD ** -0.5)).T
        row = lax.broadcasted_iota(jnp.int32, qt.shape, 0)
        for j in range(2):
            top = jnp.where((row >= j * FOX_HD) & (row < (j + 1) * FOX_HD), qt, 0.0)
            lo = j * FOX_BIAS_PIECES
            bias = jnp.where((row >= lo) & (row < lo + FOX_BIAS_PIECES), -1.0, 0.0)
            qs_ref[j] = jnp.concatenate([top, bias], axis=0).astype(BF16)
        m_ref[...] = jnp.full_like(m_ref, NEG_INF)
        l_ref[...] = jnp.zeros_like(l_ref)
        acc_ref[...] = jnp.zeros_like(acc_ref)

    def update(diagonal):
        kext = jnp.concatenate([k_ref[...], kb_ref[...]], axis=1)
        vt = vt_ref[...]
        if diagonal:
            causal = (lax.broadcasted_iota(jnp.int32, (tk, tq), 0) <= lax.broadcasted_iota(jnp.int32, (tk, tq), 1))
        for j in range(2):
            st = jnp.dot(kext, qs_ref[j], preferred_element_type=F32)
            if diagonal:
                st = jnp.where(causal, st, NEG_INF)
            m_prev = m_ref[j]
            m_new = jnp.maximum(m_prev, jnp.max(st, axis=0, keepdims=True))
            alpha = jnp.exp(m_prev - m_new)
            p = jnp.exp(st - m_new)
            l_ref[j] = alpha * l_ref[j] + jnp.sum(p, axis=0, keepdims=True)
            acc_ref[j] = alpha * acc_ref[j] + jnp.dot(vt, p.astype(BF16), preferred_element_type=F32)
            m_ref[j] = m_new

    @pl.when(ki < qi)
    def _():
        update(False)

    @pl.when(ki == qi)
    def _():
        update(True)
        row = lax.broadcasted_iota(jnp.int32, (LANES, tq), 0)
        o_ref[...] = jnp.where(row < FOX_HD, acc_ref[0] / l_ref[0], acc_ref[1] / l_ref[1]).T


def fox_attention(qg, kn, kb, vt, bd, q_gain2, bsz, seq, tq=512):
    t = qg.shape[0]
    d = FOX_HEADS * FOX_HD
    nq = seq // tq
    npair = FOX_HEADS // 2
    kblk = lambda b, h, i, j: (b * nq + jnp.minimum(i, j), h)
    return pl.pallas_call(
        _fox_attn_kernel,
        grid=(bsz, npair, nq, nq),
        in_specs=[
            pl.BlockSpec((tq, LANES), lambda b, h, i, j: (b * nq + i, h)),
            pl.BlockSpec((tq, LANES), kblk),
            pl.BlockSpec((tq, LANES), kblk),
            pl.BlockSpec((LANES, tq), lambda b, h, i, j: (h, b * nq + jnp.minimum(i, j))),
            pl.BlockSpec((LANES, LANES), lambda b, h, i, j: (0, 0)),
            pl.BlockSpec((1, LANES), lambda b, h, i, j: (0, 0)),
        ],
        out_specs=pl.BlockSpec((tq, LANES), lambda b, h, i, j: (b * nq + i, h)),
        out_shape=jax.ShapeDtypeStruct((t, d), F32),
        scratch_shapes=[
            pltpu.VMEM((2, 2 * LANES, tq), BF16),
            pltpu.VMEM((2, 1, tq), F32),
            pltpu.VMEM((2, 1, tq), F32),
            pltpu.VMEM((2, LANES, tq), F32),
        ],
        compiler_params=_params("parallel", "parallel", "parallel", "arbitrary"),
        name="fox_attention",
    )(qg, kn, kb, vt, bd, q_gain2)


SUBLANES = 8


def _merge_desc(v):
    n = len(v)
    if n == 1:
        return v
    half = n // 2
    hi = [jnp.maximum(v[i], v[i + half]) for i in range(half)]
    lo = [jnp.minimum(v[i], v[i + half]) for i in range(half)]
    return _merge_desc(hi) + _merge_desc(lo)


def _sort_desc(v):
    n = len(v)
    if n == 1:
        return v
    return _merge_desc(_sort_desc(v[:n // 2]) + _sort_desc(v[n // 2:])[::-1])


def _top16_values(groups):
    lists = _sort_desc(groups)
    shift = SUBLANES // 2
    while shift >= 1:
        partner = [pltpu.roll(a, shift, 0) for a in lists]
        if len(lists) < PEER_TOPK:
            lists = _merge_desc(lists + partner[::-1])
        else:
            n = len(lists)
            lists = _merge_desc([jnp.maximum(lists[i], partner[n - 1 - i]) for i in range(n)])
        shift //= 2
    return lists


_CAND_PAIRS = [(i, j) for i in range(PEER_TOPK) for j in range(PEER_TOPK) if (i + 1) * (j + 1) <= PEER_TOPK]


def _peer_topk_kernel(x_ref, sh_ref, sc_ref, wq_ref, sk_ref, ht_ref, p0s_ref, p1_ref, th_ref, qt_ref):
    h = _adaln_rows(x_ref[...], sh_ref[0], sc_ref[0])
    ht = h.T
    ht_ref[...] = ht.astype(BF16)
    dq = PEER_NKEYS
    ngroups = PEER_NKEYS // SUBLANES
    tb = ht.shape[1]
    sub = lax.broadcasted_iota(jnp.int32, (SUBLANES, tb), 0)
    thetas = []
    qt_ref[...] = jnp.dot(wq_ref[...], ht, preferred_element_type=F32, precision=HI)
    for head in range(PEER_HEADS):
        groups, tops = [], []
        for half in range(2):
            hp = 2 * head + half
            st = jnp.dot(sk_ref[hp], qt_ref[hp * dq:(hp + 1) * dq, :],
                         preferred_element_type=F32, precision=HI)
            g = [st[SUBLANES * k:SUBLANES * (k + 1), :] for k in range(ngroups)]
            m = functools.reduce(jnp.maximum, g)
            for shift in (4, 2, 1):
                m = jnp.maximum(m, pltpu.roll(m, shift, 0))
            g = [jnp.exp(v - m) for v in g]
            groups.append(g)
            tops.append(_top16_values(g))
        p0, p1 = tops

        def top_products(a, b):
            packed = []
            for k in range(0, len(_CAND_PAIRS), SUBLANES):
                acc = jnp.zeros((SUBLANES, tb), F32)
                for s_, (i, j) in enumerate(_CAND_PAIRS[k:k + SUBLANES]):
                    acc = jnp.where(sub == s_, a[i] * b[j], acc)
                packed.append(acc)
            while len(packed) & (len(packed) - 1):
                packed.append(jnp.zeros((SUBLANES, tb), F32))
            return _top16_values(packed)

        inv_z = 1.0 / functools.reduce(jnp.add, top_products(p0, p1))
        p0s = [v * inv_z for v in p0]
        thetas.append(top_products(p0s, p1)[PEER_TOPK - 1][0:1, :])
        for half, ref in enumerate((p0s_ref, p1_ref)):
            t16 = tops[half][PEER_TOPK - 1]
            for k in range(ngroups):
                g = groups[half][k]
                tab = jnp.where(g >= t16, g, 0.0)
                ref[head, SUBLANES * k:SUBLANES * (k + 1), :] = tab * inv_z if half == 0 else tab
    th_ref[...] = jnp.concatenate(thetas, axis=0)


def peer_topk(xf, shift, scale, wq_t, sub_keys, seq, tb=256):
    t, d = xf.shape
    per_seq = seq // tb
    nh, nk = PEER_HEADS, PEER_NKEYS
    tab = jax.ShapeDtypeStruct((nh, nk, t), F32)
    tab_spec = pl.BlockSpec((nh, nk, tb), lambda i: (0, 0, i))
    return pl.pallas_call(
        _peer_topk_kernel,
        grid=(t // tb,),
        in_specs=[
            pl.BlockSpec((tb, d), lambda i: (i, 0)),
            pl.BlockSpec((1, 1, d), lambda i: (i // per_seq, 0, 0)),
            pl.BlockSpec((1, 1, d), lambda i: (i // per_seq, 0, 0)),
            pl.BlockSpec(wq_t.shape, lambda i: (0, 0)),
            pl.BlockSpec(sub_keys.shape, lambda i: (0, 0, 0)),
        ],
        out_specs=[
            pl.BlockSpec((d, tb), lambda i: (0, i)),
            tab_spec, tab_spec,
            pl.BlockSpec((nh, tb), lambda i: (0, i)),
        ],
        out_shape=[jax.ShapeDtypeStruct((d, t), BF16), tab, tab, jax.ShapeDtypeStruct((nh, t), F32)],
        scratch_shapes=[pltpu.VMEM((wq_t.shape[0], tb), F32)],
        compiler_params=_params("parallel"),
        name="peer_topk",
    )(xf, shift, scale, wq_t, sub_keys)


def _peer_dense_kernel(ht_ref, u_ref, vt_ref, p0s_ref, p1_ref, th_ref, x_ref, g2_ref, out_ref,
                       at_ref, wt_ref, acc_ref, *, tiles):
    e = pl.program_id(1)
    nk = PEER_NKEYS
    slot = e % 2
    prev = 1 - slot

    @pl.when(e == 0)
    def _():
        acc_ref[...] = jnp.zeros_like(acc_ref)
        at_ref[0] = jnp.dot(u_ref[...], ht_ref[...], preferred_element_type=F32)

    @pl.when(e > 0)
    def _():
        a_new = jnp.dot(u_ref[...], ht_ref[...], preferred_element_type=F32)
        key_base = (e - 1) * tiles
        for i in range(tiles):
            rows = slice(i * nk, (i + 1) * nk)
            a = at_ref[prev, rows, :]
            act = 0.5 * a * (1.0 + lax.erf(a * (1.0 / math.sqrt(2.0))))
            g = jnp.zeros_like(a)
            for head in range(PEER_HEADS):
                w = p0s_ref[head, pl.ds(key_base + i, 1), :] * p1_ref[head]
                g = jnp.where(w >= th_ref[pl.ds(head, 1), :], g + w, g)
            wt_ref[rows, :] = (g * act).astype(BF16)
        acc_ref[...] += jnp.dot(vt_ref[...], wt_ref[...], preferred_element_type=F32)
        at_ref[slot] = a_new

    @pl.when(e == pl.num_programs(1) - 1)
    def _():
        out_ref[...] = x_ref[...] + g2_ref[0] * acc_ref[...].T


def peer_dense(ht, u_bf16, vt_bf16, p0s, p1, theta, xf, g2, seq, tb=256, ec=1024):
    t, d = xf.shape
    ne = u_bf16.shape[0]
    nchunk = ne // ec
    per_seq = seq // tb
    nh, nk = PEER_HEADS, PEER_NKEYS
    tab_spec = pl.BlockSpec((nh, nk, tb), lambda i, e: (0, 0, i))
    return pl.pallas_call(
        functools.partial(_peer_dense_kernel, tiles=ec // nk),
        grid=(t // tb, nchunk + 1),
        in_specs=[
            pl.BlockSpec((d, tb), lambda i, e: (0, i)),
            pl.BlockSpec((ec, d), lambda i, e: (jnp.minimum(e, nchunk - 1), 0)),
            pl.BlockSpec((d, ec), lambda i, e: (0, jnp.maximum(e - 1, 0))),
            tab_spec, tab_spec,
            pl.BlockSpec((nh, tb), lambda i, e: (0, i)),
            pl.BlockSpec((tb, d), lambda i, e: (i, 0)),
            pl.BlockSpec((1, 1, d), lambda i, e: (i // per_seq, 0, 0)),
        ],
        out_specs=pl.BlockSpec((tb, d), lambda i, e: (i, 0)),
        out_shape=jax.ShapeDtypeStruct((t, d), F32),
        scratch_shapes=[
            pltpu.VMEM((2, ec, tb), F32),
            pltpu.VMEM((ec, tb), BF16),
            pltpu.VMEM((d, tb), F32),
        ],
        compiler_params=_params("parallel", "arbitrary"),
        name="peer_dense",
    )(ht, u_bf16, vt_bf16, p0s, p1, theta, xf, g2)


def _pad_cols(w, n):
    return jnp.pad(w, ((0, 0), (0, n - w.shape[1])))


def kernel(x, c, mod_w, mod_b, gla_w_in, gla_w_gate2, gla_b_gate, gla_o_norm, gla_w_out, kv_mod_w, kv_mod_b, fox_w_kvf, fox_b_f, fox_k_norm, fox_w_qg, fox_q_norm, fox_w_out, peer_w_q, peer_sub_keys, peer_u, peer_v):
    bsz, seq, d = x.shape
    depth = mod_w.shape[0]
    n_gla = gla_w_in.shape[0]
    xf = x.reshape(bsz * seq, d)
    gla_dk = GLA_HEADS * GLA_HK
    gla_main = 2 * gla_dk + 2 * GLA_HEADS * GLA_HV

    shared = None
    for layer in range(depth):
        if layer == n_gla:
            kv_mod = mod_matmul(c, kv_mod_w, kv_mod_b).reshape(bsz, 2, 1, d)
            kv_sh, kv_sc = kv_mod[:, 0], kv_mod[:, 1]
            kv = adaln_matmul(xf, kv_sh, kv_sc, fox_w_kvf[:, :2 * d].astype(BF16), seq)
            pos = jnp.arange(d) % LANES
            src = 2 * (jnp.arange(d) // LANES) + pos // FOX_BIAS_PIECES
            used = pos < FOX_BIAS_LANES
            src = jnp.where(used, src, 0)
            w_f = jnp.where(used[None, :], fox_w_kvf[:, 2 * d:][:, src], 0.0)
            b_f = jnp.where(used, fox_b_f[src], 0.0).reshape(1, d)
            f_ext = adaln_matmul(xf, kv_sh, kv_sc, w_f, seq)
            kb = forget_cumsum(f_ext, b_f, bsz, seq)
            head_avg = jnp.kron(jnp.eye(LANES // FOX_HD, dtype=F32), jnp.full((FOX_HD, FOX_HD), 1.0 / FOX_HD, F32))
            kn, vt = fox_kv_prep(kv, head_avg, jnp.tile(fox_k_norm, LANES // FOX_HD).reshape(1, LANES))
            shared = (kn, kb, vt, head_avg)

        mod = mod_matmul(c, mod_w[layer], mod_b[layer]).reshape(bsz, 6, 1, d)
        sh1, sc1, g1, sh2, sc2, g2 = (mod[:, i] for i in range(6))

        if layer < n_gla:
            w_in = gla_w_in[layer]
            proj = adaln_matmul(xf, sh1, sc1, w_in[:, :gla_main].astype(BF16), seq)
            glow = adaln_matmul(xf, sh1, sc1, _pad_cols(w_in[:, gla_main:], LANES), seq)
            w2p = jnp.pad(gla_w_gate2[layer], ((0, LANES - GLA_GATE_RANK), (0, 0)))
            o = gla_recurrence(proj, glow, w2p, gla_b_gate[layer].reshape(1, gla_dk), bsz, seq)
            xf = mixer_out(o, proj, (2 * gla_dk + GLA_HEADS * GLA_HV) // d, xf, g1,
                           gla_w_out[layer].astype(BF16), seq, o_norm=gla_o_norm[layer])
        else:
            j = layer - n_gla
            kn, kb, vt, head_avg = shared
            qg = adaln_matmul(xf, sh1, sc1, fox_w_qg[j].astype(BF16), seq)
            o = fox_attention(qg, kn, kb, vt, head_avg,
                              jnp.tile(fox_q_norm[j], LANES // FOX_HD).reshape(1, LANES), bsz, seq)
            xf = mixer_out(o, qg, 1, xf, g1, fox_w_out[j].astype(BF16), seq)

        sub_keys = peer_sub_keys[layer].reshape(2 * PEER_HEADS, PEER_NKEYS, -1)
        ht, p0s, p1, theta = peer_topk(xf, sh2, sc2, peer_w_q[layer].T, sub_keys, seq)
        xf = peer_dense(ht, peer_u[layer].astype(BF16), peer_v[layer].T.astype(BF16),
                        p0s, p1, theta, xf, g2, seq)
    return xf.reshape(bsz, seq, d)
```

```python
import functools
import math

import jax
import jax.numpy as jnp
from jax import lax
from jax.experimental import pallas as pl
from jax.experimental.pallas import tpu as pltpu

F32 = jnp.float32
BF16 = jnp.bfloat16
HI = lax.Precision.HIGHEST
EPS = 1e-6
NEG_INF = float("-inf")

VMEM_LIMIT_BYTES = 48 * 1024 * 1024
LANES = 128

GLA_HEADS = 4
GLA_HK = 128
GLA_HV = 256
GLA_GATE_RANK = 16
GLA_GATE_NORM = 16.0
GLA_CHUNK = 64

FOX_HD = 64
FOX_HEADS = 16

PEER_HEADS = 8
PEER_NKEYS = 128
PEER_TOPK = 16
PEER_CHUNK = 2048


def _params(*sem):
    return pltpu.CompilerParams(dimension_semantics=sem, vmem_limit_bytes=VMEM_LIMIT_BYTES)


def _adaln_rows(x, shift, scale):
    ms = jnp.mean(x * x, axis=-1, keepdims=True)
    return x * lax.rsqrt(ms + EPS) * (1.0 + scale) + shift


def _mod_kernel(c_ref, w_ref, b_ref, o_ref):
    o_ref[...] = jnp.dot(c_ref[...], w_ref[...], preferred_element_type=F32, precision=HI) + b_ref[...]


def mod_matmul(c, w, b, tn=512):
    bsz, d = c.shape
    n = w.shape[1]
    return pl.pallas_call(
        _mod_kernel,
        grid=(n // tn,),
        in_specs=[
            pl.BlockSpec((bsz, d), lambda j: (0, 0)),
            pl.BlockSpec((d, tn), lambda j: (0, j)),
            pl.BlockSpec((1, tn), lambda j: (0, j)),
        ],
        out_specs=pl.BlockSpec((bsz, tn), lambda j: (0, j)),
        out_shape=jax.ShapeDtypeStruct((bsz, n), F32),
        compiler_params=_params("parallel"),
        name="mod_matmul",
    )(c, w, b.reshape(1, n))


def _adaln_mm_kernel(x_ref, sh_ref, sc_ref, w_ref, o_ref, h_ref):
    @pl.when(pl.program_id(1) == 0)
    def _():
        h_ref[...] = _adaln_rows(x_ref[...], sh_ref[0], sc_ref[0]).astype(h_ref.dtype)

    if w_ref.dtype == BF16:
        acc = jnp.dot(h_ref[...], w_ref[...], preferred_element_type=F32)
    else:
        acc = jnp.dot(h_ref[...], w_ref[...], preferred_element_type=F32, precision=HI)
    o_ref[...] = acc.astype(o_ref.dtype)


def adaln_matmul(xf, shift, scale, w, seq, tm=512, tn=512):
    t, d = xf.shape
    n = w.shape[1]
    tn = min(tn, n)
    per_seq = seq // tm
    return pl.pallas_call(
        _adaln_mm_kernel,
        grid=(t // tm, n // tn),
        in_specs=[
            pl.BlockSpec((tm, d), lambda i, j: (i, 0)),
            pl.BlockSpec((1, 1, d), lambda i, j: (i // per_seq, 0, 0)),
            pl.BlockSpec((1, 1, d), lambda i, j: (i // per_seq, 0, 0)),
            pl.BlockSpec((d, tn), lambda i, j: (0, j)),
        ],
        out_specs=pl.BlockSpec((tm, tn), lambda i, j: (i, j)),
        out_shape=jax.ShapeDtypeStruct((t, n), F32),
        scratch_shapes=[pltpu.VMEM((tm, d), w.dtype)],
        compiler_params=_params("parallel", "arbitrary"),
        name="adaln_matmul",
    )(xf, shift, scale, w)


GLA_SUB = 16


def _gla_kernel(q_ref, k_ref, v_ref, gl_ref, w2_ref, bg_ref, o_ref, st_ref, *, nchunk):
    c, sb = GLA_CHUNK, GLA_SUB
    nt = (((1,), (1,)), ((), ()))

    @pl.when(pl.program_id(2) == 0)
    def _():
        st_ref[...] = jnp.zeros_like(st_ref)

    row_c = lax.broadcasted_iota(jnp.int32, (c, GLA_HK), 0)
    row_b = lax.broadcasted_iota(jnp.int32, (sb, GLA_HK), 0)
    lane_b = lax.broadcasted_iota(jnp.int32, (sb, GLA_HK), 1)
    tri = (lax.broadcasted_iota(jnp.int32, (c, c), 1) <= lax.broadcasted_iota(jnp.int32, (c, c), 0)).astype(F32)

    def chunk(ci, carry):
        r0 = pl.multiple_of(ci * c, c)
        q = q_ref[pl.ds(r0, c), :] * (GLA_HK ** -0.5)
        k = k_ref[pl.ds(r0, c), :]
        v = v_ref[pl.ds(r0, c), :].astype(BF16)
        pre = jnp.dot(gl_ref[pl.ds(r0, c), :], w2_ref[...], preferred_element_type=F32, precision=HI) + bg_ref[...]
        gk = jax.nn.log_sigmoid(pre) * (1.0 / GLA_GATE_NORM)
        b = jnp.dot(tri, gk, preferred_element_type=F32, precision=HI)
        st = st_ref[...]
        o_inter = lax.dot_general((q * jnp.exp(b)).astype(BF16), st.astype(BF16), nt, preferred_element_type=F32)

        blocks = []
        for i in range(c // sb):
            lo = i * sb
            b_i, q_i, k_i = b[lo:lo + sb], q[lo:lo + sb], k[lo:lo + sb]
            d = jnp.zeros((sb, GLA_HK), F32)
            for j in range(sb):
                rel = jnp.where(row_b >= j, b_i - b_i[j:j + 1], NEG_INF)
                m = jnp.exp(rel) * (q_i * k_i[j:j + 1])
                d = jnp.where(lane_b == lo + j, jnp.sum(m, axis=-1, keepdims=True), d)
            att = d[:, :c]
            if i > 0:
                ref = b_i[0:1]
                qs = q_i * jnp.exp(b_i - ref)
                ks = k * jnp.exp(jnp.where(row_c < lo, ref - b, NEG_INF))
                att = att + lax.dot_general(qs.astype(BF16), ks.astype(BF16), nt, preferred_element_type=F32)
            blocks.append(att)
        attn = jnp.concatenate(blocks, axis=0)
        o_ref[pl.ds(r0, c), :] = o_inter + jnp.dot(attn.astype(BF16), v, preferred_element_type=F32)
        b_last = b[c - 1:c, :]
        kdec = (k * jnp.exp(b_last - b)).astype(BF16)
        st_ref[...] = st * jnp.exp(b_last) + lax.dot_general(
            v, kdec, (((0,), (0,)), ((), ())), preferred_element_type=F32)
        return carry

    lax.fori_loop(0, nchunk, chunk, 0)


def gla_recurrence(proj, glow, w2p, bg, bsz, seq, ts=512):
    t = proj.shape[0]
    ns = seq // ts
    kcol = (GLA_HEADS * GLA_HK) // GLA_HK
    vcol = (2 * GLA_HEADS * GLA_HK) // GLA_HV
    return pl.pallas_call(
        functools.partial(_gla_kernel, nchunk=ts // GLA_CHUNK),
        grid=(bsz, GLA_HEADS, ns),
        in_specs=[
            pl.BlockSpec((ts, GLA_HK), lambda b, h, s: (b * ns + s, h)),
            pl.BlockSpec((ts, GLA_HK), lambda b, h, s: (b * ns + s, kcol + h)),
            pl.BlockSpec((ts, GLA_HV), lambda b, h, s: (b * ns + s, vcol + h)),
            pl.BlockSpec((ts, LANES), lambda b, h, s: (b * ns + s, 0)),
            pl.BlockSpec((LANES, GLA_HK), lambda b, h, s: (0, h)),
            pl.BlockSpec((1, GLA_HK), lambda b, h, s: (0, h)),
        ],
        out_specs=pl.BlockSpec((ts, GLA_HV), lambda b, h, s: (b * ns + s, h)),
        out_shape=jax.ShapeDtypeStruct((t, GLA_HEADS * GLA_HV), F32),
        scratch_shapes=[pltpu.VMEM((GLA_HV, GLA_HK), F32)],
        compiler_params=_params("parallel", "parallel", "arbitrary"),
        name="gla_recurrence",
    )(proj, proj, proj, glow, w2p, bg)


def _gla_out_kernel(o_ref, g_ref, x_ref, g1_ref, on_ref, w_ref, out_ref):
    parts = []
    for h in range(GLA_HEADS):
        oh = o_ref[:, h * GLA_HV:(h + 1) * GLA_HV]
        ms = jnp.mean(oh * oh, axis=-1, keepdims=True)
        parts.append(oh * lax.rsqrt(ms + EPS) * on_ref[...])
    y = jnp.concatenate(parts, axis=-1) * jax.nn.silu(g_ref[...])
    mix = jnp.dot(y.astype(BF16), w_ref[...], preferred_element_type=F32)
    out_ref[...] = x_ref[...] + g1_ref[0] * mix


def _fox_out_kernel(o_ref, g_ref, x_ref, g1_ref, w_ref, out_ref):
    y = o_ref[...] * jax.nn.sigmoid(g_ref[...])
    mix = jnp.dot(y.astype(BF16), w_ref[...], preferred_element_type=F32)
    out_ref[...] = x_ref[...] + g1_ref[0] * mix


def mixer_out(o, gsrc, gcol, xf, g1, w_bf16, seq, o_norm=None, tm=512):
    t, d = xf.shape
    per_seq = seq // tm
    row = lambda i: (i, 0)
    in_specs = [
        pl.BlockSpec((tm, d), row),
        pl.BlockSpec((tm, d), lambda i: (i, gcol)),
        pl.BlockSpec((tm, d), row),
        pl.BlockSpec((1, 1, d), lambda i: (i // per_seq, 0, 0)),
    ]
    args = [o, gsrc, xf, g1]
    if o_norm is not None:
        in_specs.append(pl.BlockSpec((1, GLA_HV), lambda i: (0, 0)))
        args.append(o_norm.reshape(1, GLA_HV))
        body = _gla_out_kernel
    else:
        body = _fox_out_kernel
    in_specs.append(pl.BlockSpec((d, d), lambda i: (0, 0)))
    args.append(w_bf16)
    return pl.pallas_call(
        body,
        grid=(t // tm,),
        in_specs=in_specs,
        out_specs=pl.BlockSpec((tm, d), row),
        out_shape=jax.ShapeDtypeStruct((t, d), F32),
        compiler_params=_params("parallel"),
        name="mixer_out",
    )(*args)


FOX_BIAS_PIECES = 3
FOX_BIAS_LANES = 2 * FOX_BIAS_PIECES


def _fgate_kernel(f_ref, bf_ref, kb_ref, carry_ref):
    tc = f_ref.shape[0]

    @pl.when(pl.program_id(1) == 0)
    def _():
        carry_ref[...] = jnp.zeros_like(carry_ref)

    logf = jax.nn.log_sigmoid(f_ref[...] + bf_ref[...])
    tri = (lax.broadcasted_iota(jnp.int32, (tc, tc), 1) <= lax.broadcasted_iota(jnp.int32, (tc, tc), 0)).astype(F32)
    cum = jnp.dot(tri, logf, preferred_element_type=F32, precision=HI) + carry_ref[...]
    carry_ref[...] = cum[tc - 1:tc, :]
    hi = cum.astype(BF16).astype(F32)
    mid = (cum - hi).astype(BF16).astype(F32)
    lo = cum - hi - mid
    pos = lax.broadcasted_iota(jnp.int32, cum.shape, 1) % LANES
    piece = pos % FOX_BIAS_PIECES
    out = jnp.where(piece == 0, hi, jnp.where(piece == 1, mid, lo))
    kb_ref[...] = jnp.where(pos < FOX_BIAS_LANES, out, 0.0).astype(BF16)


def forget_cumsum(f_ext, bf_ext, bsz, seq, tc=256):
    t, n = f_ext.shape
    ns = seq // tc
    return pl.pallas_call(
        _fgate_kernel,
        grid=(bsz, ns),
        in_specs=[
            pl.BlockSpec((tc, n), lambda b, s: (b * ns + s, 0)),
            pl.BlockSpec((1, n), lambda b, s: (0, 0)),
        ],
        out_specs=pl.BlockSpec((tc, n), lambda b, s: (b * ns + s, 0)),
        out_shape=jax.ShapeDtypeStruct((t, n), BF16),
        scratch_shapes=[pltpu.VMEM((1, n), F32)],
        compiler_params=_params("parallel", "arbitrary"),
        name="forget_cumsum",
    )(f_ext, bf_ext)


def _pair_headnorm(x2, bd, gain2):
    ms = jnp.dot(x2 * x2, bd, preferred_element_type=F32, precision=HI)
    return x2 * lax.rsqrt(ms + EPS) * gain2


def _kv_prep_kernel(kv_ref, bd_ref, kn_ref, ko_ref, vt_ref):
    d = ko_ref.shape[1]
    for cb in range(d // LANES):
        cols = slice(cb * LANES, (cb + 1) * LANES)
        ko_ref[:, cols] = _pair_headnorm(kv_ref[:, cols], bd_ref[...], kn_ref[...]).astype(BF16)
        vt_ref[cols, :] = kv_ref[:, d + cb * LANES:d + (cb + 1) * LANES].T.astype(BF16)


def fox_kv_prep(kv, bd, k_gain2, tm=512):
    t, d2 = kv.shape
    d = d2 // 2
    return pl.pallas_call(
        _kv_prep_kernel,
        grid=(t // tm,),
        in_specs=[
            pl.BlockSpec((tm, d2), lambda i: (i, 0)),
            pl.BlockSpec((LANES, LANES), lambda i: (0, 0)),
            pl.BlockSpec((1, LANES), lambda i: (0, 0)),
        ],
        out_specs=[pl.BlockSpec((tm, d), lambda i: (i, 0)), pl.BlockSpec((d, tm), lambda i: (0, i))],
        out_shape=[jax.ShapeDtypeStruct((t, d), BF16), jax.ShapeDtypeStruct((d, t), BF16)],
        compiler_params=_params("parallel"),
        name="fox_kv_prep",
    )(kv, bd, k_gain2)


def _fox_attn_kernel(q_ref, k_ref, kb_ref, vt_ref, bd_ref, qn_ref, o_ref, qs_ref, m_ref, l_ref, acc_ref):
    qi = pl.program_id(2)
    ki = pl.program_id(3)
    tq = q_ref.shape[0]
    tk = k_ref.shape[0]

    @pl.when(ki == 0)
    def _():
        qt = (_pair_headnorm(q_ref[...], bd_ref[...], qn_ref[...]) * (FOX_HD ** -0.5)).T
        row = lax.broadcasted_iota(jnp.int32, qt.shape, 0)
        for j in range(2):
            top = jnp.where((row >= j * FOX_HD) & (row < (j + 1) * FOX_HD), qt, 0.0)
            lo = j * FOX_BIAS_PIECES
            bias = jnp.where((row >= lo) & (row < lo + FOX_BIAS_PIECES), -1.0, 0.0)
            qs_ref[j] = jnp.concatenate([top, bias], axis=0).astype(BF16)
        m_ref[...] = jnp.full_like(m_ref, NEG_INF)
        l_ref[...] = jnp.zeros_like(l_ref)
        acc_ref[...] = jnp.zeros_like(acc_ref)

    def update(diagonal):
        kext = jnp.concatenate([k_ref[...], kb_ref[...]], axis=1)
        vt = vt_ref[...]
        if diagonal:
            causal = (lax.broadcasted_iota(jnp.int32, (tk, tq), 0) <= lax.broadcasted_iota(jnp.int32, (tk, tq), 1))
        for j in range(2):
            st = jnp.dot(kext, qs_ref[j], preferred_element_type=F32)
            if diagonal:
                st = jnp.where(causal, st, NEG_INF)
            m_prev = m_ref[j]
            m_new = jnp.maximum(m_prev, jnp.max(st, axis=0, keepdims=True))
            alpha = jnp.exp(m_prev - m_new)
            p = jnp.exp(st - m_new)
            l_ref[j] = alpha * l_ref[j] + jnp.sum(p, axis=0, keepdims=True)
            acc_ref[j] = alpha * acc_ref[j] + jnp.dot(vt, p.astype(BF16), preferred_element_type=F32)
            m_ref[j] = m_new

    @pl.when(ki < qi)
    def _():
        update(False)

    @pl.when(ki == qi)
    def _():
        update(True)
        row = lax.broadcasted_iota(jnp.int32, (LANES, tq), 0)
        o_ref[...] = jnp.where(row < FOX_HD, acc_ref[0] / l_ref[0], acc_ref[1] / l_ref[1]).T


def fox_attention(qg, kn, kb, vt, bd, q_gain2, bsz, seq, tq=512):
    t = qg.shape[0]
    d = FOX_HEADS * FOX_HD
    nq = seq // tq
    npair = FOX_HEADS // 2
    kblk = lambda b, h, i, j: (b * nq + jnp.minimum(i, j), h)
    return pl.pallas_call(
        _fox_attn_kernel,
        grid=(bsz, npair, nq, nq),
        in_specs=[
            pl.BlockSpec((tq, LANES), lambda b, h, i, j: (b * nq + i, h)),
            pl.BlockSpec((tq, LANES), kblk),
            pl.BlockSpec((tq, LANES), kblk),
            pl.BlockSpec((LANES, tq), lambda b, h, i, j: (h, b * nq + jnp.minimum(i, j))),
            pl.BlockSpec((LANES, LANES), lambda b, h, i, j: (0, 0)),
            pl.BlockSpec((1, LANES), lambda b, h, i, j: (0, 0)),
        ],
        out_specs=pl.BlockSpec((tq, LANES), lambda b, h, i, j: (b * nq + i, h)),
        out_shape=jax.ShapeDtypeStruct((t, d), F32),
        scratch_shapes=[
            pltpu.VMEM((2, 2 * LANES, tq), BF16),
            pltpu.VMEM((2, 1, tq), F32),
            pltpu.VMEM((2, 1, tq), F32),
            pltpu.VMEM((2, LANES, tq), F32),
        ],
        compiler_params=_params("parallel", "parallel", "parallel", "arbitrary"),
        name="fox_attention",
    )(qg, kn, kb, vt, bd, q_gain2)


SUBLANES = 8


def _merge_desc(v):
    n = len(v)
    if n == 1:
        return v
    half = n // 2
    hi = [jnp.maximum(v[i], v[i + half]) for i in range(half)]
    lo = [jnp.minimum(v[i], v[i + half]) for i in range(half)]
    return _merge_desc(hi) + _merge_desc(lo)


def _sort_desc(v):
    n = len(v)
    if n == 1:
        return v
    return _merge_desc(_sort_desc(v[:n // 2]) + _sort_desc(v[n // 2:])[::-1])


def _top16_values(groups):
    lists = _sort_desc(groups)
    shift = SUBLANES // 2
    while shift >= 1:
        partner = [pltpu.roll(a, shift, 0) for a in lists]
        if len(lists) < PEER_TOPK:
            lists = _merge_desc(lists + partner[::-1])
        else:
            n = len(lists)
            lists = _merge_desc([jnp.maximum(lists[i], partner[n - 1 - i]) for i in range(n)])
        shift //= 2
    return lists


_CAND_PAIRS = [(i, j) for i in range(PEER_TOPK) for j in range(PEER_TOPK) if (i + 1) * (j + 1) <= PEER_TOPK]


def _split_bf16(x):
    hi = x.astype(BF16)
    return hi, (x - hi.astype(F32)).astype(BF16)


def _peer_topk_kernel(x_ref, sh_ref, sc_ref, wqh_ref, wql_ref, sk_ref, ht_ref, p0s_ref, p1_ref, th_ref, qt_ref):
    h = _adaln_rows(x_ref[...], sh_ref[0], sc_ref[0])
    ht = h.T
    ht_ref[...] = ht.astype(BF16)
    dq = PEER_NKEYS
    ngroups = PEER_NKEYS // SUBLANES
    tb = ht.shape[1]
    sub = lax.broadcasted_iota(jnp.int32, (SUBLANES, tb), 0)
    thetas = []
    ht_hi, ht_lo = _split_bf16(ht)
    qt_ref[...] = (jnp.dot(wqh_ref[...], ht_hi, preferred_element_type=F32)
                   + jnp.dot(wqh_ref[...], ht_lo, preferred_element_type=F32)
                   + jnp.dot(wql_ref[...], ht_hi, preferred_element_type=F32))
    for head in range(PEER_HEADS):
        groups, tops = [], []
        for half in range(2):
            hp = 2 * head + half
            st = jnp.dot(sk_ref[hp], qt_ref[hp * dq:(hp + 1) * dq, :],
                         preferred_element_type=F32, precision=HI)
            g = [st[SUBLANES * k:SUBLANES * (k + 1), :] for k in range(ngroups)]
            m = functools.reduce(jnp.maximum, g)
            for shift in (4, 2, 1):
                m = jnp.maximum(m, pltpu.roll(m, shift, 0))
            g = [jnp.exp(v - m) for v in g]
            groups.append(g)
            tops.append(_top16_values(g))
        p0, p1 = tops

        def top_products(a, b):
            packed = []
            for k in range(0, len(_CAND_PAIRS), SUBLANES):
                acc = jnp.zeros((SUBLANES, tb), F32)
                for s_, (i, j) in enumerate(_CAND_PAIRS[k:k + SUBLANES]):
                    acc = jnp.where(sub == s_, a[i] * b[j], acc)
                packed.append(acc)
            while len(packed) & (len(packed) - 1):
                packed.append(jnp.zeros((SUBLANES, tb), F32))
            return _top16_values(packed)

        inv_z = 1.0 / functools.reduce(jnp.add, top_products(p0, p1))
        p0s = [v * inv_z for v in p0]
        thetas.append(top_products(p0s, p1)[PEER_TOPK - 1][0:1, :])
        for half, ref in enumerate((p0s_ref, p1_ref)):
            t16 = tops[half][PEER_TOPK - 1]
            for k in range(ngroups):
                g = groups[half][k]
                tab = jnp.where(g >= t16, g, 0.0)
                ref[head, SUBLANES * k:SUBLANES * (k + 1), :] = tab * inv_z if half == 0 else tab
    th_ref[...] = jnp.concatenate(thetas, axis=0)


def peer_topk(xf, shift, scale, wq_t, sub_keys, seq, tb=256):
    wq_hi, wq_lo = _split_bf16(wq_t)
    t, d = xf.shape
    per_seq = seq // tb
    nh, nk = PEER_HEADS, PEER_NKEYS
    tab = jax.ShapeDtypeStruct((nh, nk, t), F32)
    tab_spec = pl.BlockSpec((nh, nk, tb), lambda i: (0, 0, i))
    return pl.pallas_call(
        _peer_topk_kernel,
        grid=(t // tb,),
        in_specs=[
            pl.BlockSpec((tb, d), lambda i: (i, 0)),
            pl.BlockSpec((1, 1, d), lambda i: (i // per_seq, 0, 0)),
            pl.BlockSpec((1, 1, d), lambda i: (i // per_seq, 0, 0)),
            pl.BlockSpec(wq_t.shape, lambda i: (0, 0)),
            pl.BlockSpec(wq_t.shape, lambda i: (0, 0)),
            pl.BlockSpec(sub_keys.shape, lambda i: (0, 0, 0)),
        ],
        out_specs=[
            pl.BlockSpec((d, tb), lambda i: (0, i)),
            tab_spec, tab_spec,
            pl.BlockSpec((nh, tb), lambda i: (0, i)),
        ],
        out_shape=[jax.ShapeDtypeStruct((d, t), BF16), tab, tab, jax.ShapeDtypeStruct((nh, t), F32)],
        scratch_shapes=[pltpu.VMEM((wq_t.shape[0], tb), F32)],
        compiler_params=_params("parallel"),
        name="peer_topk",
    )(xf, shift, scale, wq_hi, wq_lo, sub_keys)


def _peer_dense_kernel(ht_ref, u_ref, vt_ref, p0s_ref, p1_ref, th_ref, x_ref, g2_ref, out_ref,
                       at_ref, wt_ref, acc_ref, *, tiles):
    e = pl.program_id(1)
    nk = PEER_NKEYS
    slot = e % 2
    prev = 1 - slot

    @pl.when(e == 0)
    def _():
        acc_ref[...] = jnp.zeros_like(acc_ref)
        at_ref[0] = jnp.dot(u_ref[...], ht_ref[...], preferred_element_type=F32)

    @pl.when(e > 0)
    def _():
        a_new = jnp.dot(u_ref[...], ht_ref[...], preferred_element_type=F32)
        key_base = (e - 1) * tiles
        for i in range(tiles):
            rows = slice(i * nk, (i + 1) * nk)
            a = at_ref[prev, rows, :]
            act = 0.5 * a * (1.0 + lax.erf(a * (1.0 / math.sqrt(2.0))))
            g = jnp.zeros_like(a)
            for head in range(PEER_HEADS):
                w = p0s_ref[head, pl.ds(key_base + i, 1), :] * p1_ref[head]
                g = jnp.where(w >= th_ref[pl.ds(head, 1), :], g + w, g)
            wt_ref[rows, :] = (g * act).astype(BF16)
        acc_ref[...] += jnp.dot(vt_ref[0], wt_ref[...], preferred_element_type=F32)
        at_ref[slot] = a_new

    @pl.when(e == pl.num_programs(1) - 1)
    def _():
        out_ref[...] = x_ref[...] + g2_ref[0] * acc_ref[...].T


def peer_dense(ht, u_bf16, vt_chunks, p0s, p1, theta, xf, g2, seq, tb=256):
    t, d = xf.shape
    ne = u_bf16.shape[0]
    nchunk, _, ec = vt_chunks.shape
    per_seq = seq // tb
    nh, nk = PEER_HEADS, PEER_NKEYS
    tab_spec = pl.BlockSpec((nh, nk, tb), lambda i, e: (0, 0, i))
    return pl.pallas_call(
        functools.partial(_peer_dense_kernel, tiles=ec // nk),
        grid=(t // tb, nchunk + 1),
        in_specs=[
            pl.BlockSpec((d, tb), lambda i, e: (0, i)),
            pl.BlockSpec((ec, d), lambda i, e: (jnp.minimum(e, nchunk - 1), 0)),
            pl.BlockSpec((1, d, ec), lambda i, e: (jnp.maximum(e - 1, 0), 0, 0)),
            tab_spec, tab_spec,
            pl.BlockSpec((nh, tb), lambda i, e: (0, i)),
            pl.BlockSpec((tb, d), lambda i, e: (i, 0)),
            pl.BlockSpec((1, 1, d), lambda i, e: (i // per_seq, 0, 0)),
        ],
        out_specs=pl.BlockSpec((tb, d), lambda i, e: (i, 0)),
        out_shape=jax.ShapeDtypeStruct((t, d), F32),
        scratch_shapes=[
            pltpu.VMEM((2, ec, tb), F32),
            pltpu.VMEM((ec, tb), BF16),
            pltpu.VMEM((d, tb), F32),
        ],
        compiler_params=_params("parallel", "arbitrary"),
        name="peer_dense",
    )(ht, u_bf16, vt_chunks, p0s, p1, theta, xf, g2)


def _pad_cols(w, n):
    return jnp.pad(w, ((0, 0), (0, n - w.shape[1])))


def kernel(x, c, mod_w, mod_b, gla_w_in, gla_w_gate2, gla_b_gate, gla_o_norm, gla_w_out, kv_mod_w, kv_mod_b, fox_w_kvf, fox_b_f, fox_k_norm, fox_w_qg, fox_q_norm, fox_w_out, peer_w_q, peer_sub_keys, peer_u, peer_v):
    bsz, seq, d = x.shape
    depth = mod_w.shape[0]
    n_gla = gla_w_in.shape[0]
    xf = x.reshape(bsz * seq, d)
    gla_dk = GLA_HEADS * GLA_HK
    gla_main = 2 * gla_dk + 2 * GLA_HEADS * GLA_HV

    shared = None
    for layer in range(depth):
        if layer == n_gla:
            kv_mod = mod_matmul(c, kv_mod_w, kv_mod_b).reshape(bsz, 2, 1, d)
            kv_sh, kv_sc = kv_mod[:, 0], kv_mod[:, 1]
            kv = adaln_matmul(xf, kv_sh, kv_sc, fox_w_kvf[:, :2 * d].astype(BF16), seq)
            pos = jnp.arange(d) % LANES
            src = 2 * (jnp.arange(d) // LANES) + pos // FOX_BIAS_PIECES
            used = pos < FOX_BIAS_LANES
            src = jnp.where(used, src, 0)
            w_f = jnp.where(used[None, :], fox_w_kvf[:, 2 * d:][:, src], 0.0)
            b_f = jnp.where(used, fox_b_f[src], 0.0).reshape(1, d)
            f_ext = adaln_matmul(xf, kv_sh, kv_sc, w_f, seq)
            kb = forget_cumsum(f_ext, b_f, bsz, seq)
            head_avg = jnp.kron(jnp.eye(LANES // FOX_HD, dtype=F32), jnp.full((FOX_HD, FOX_HD), 1.0 / FOX_HD, F32))
            kn, vt = fox_kv_prep(kv, head_avg, jnp.tile(fox_k_norm, LANES // FOX_HD).reshape(1, LANES))
            shared = (kn, kb, vt, head_avg)

        mod = mod_matmul(c, mod_w[layer], mod_b[layer]).reshape(bsz, 6, 1, d)
        sh1, sc1, g1, sh2, sc2, g2 = (mod[:, i] for i in range(6))

        if layer < n_gla:
            w_in = gla_w_in[layer]
            proj = adaln_matmul(xf, sh1, sc1, w_in[:, :gla_main].astype(BF16), seq)
            glow = adaln_matmul(xf, sh1, sc1, _pad_cols(w_in[:, gla_main:], LANES), seq)
            w2p = jnp.pad(gla_w_gate2[layer], ((0, LANES - GLA_GATE_RANK), (0, 0)))
            o = gla_recurrence(proj, glow, w2p, gla_b_gate[layer].reshape(1, gla_dk), bsz, seq)
            xf = mixer_out(o, proj, (2 * gla_dk + GLA_HEADS * GLA_HV) // d, xf, g1,
                           gla_w_out[layer].astype(BF16), seq, o_norm=gla_o_norm[layer])
        else:
            j = layer - n_gla
            kn, kb, vt, head_avg = shared
            qg = adaln_matmul(xf, sh1, sc1, fox_w_qg[j].astype(BF16), seq)
            o = fox_attention(qg, kn, kb, vt, head_avg,
                              jnp.tile(fox_q_norm[j], LANES // FOX_HD).reshape(1, LANES), bsz, seq)
            xf = mixer_out(o, qg, 1, xf, g1, fox_w_out[j].astype(BF16), seq)

        sub_keys = peer_sub_keys[layer].reshape(2 * PEER_HEADS, PEER_NKEYS, -1)
        ht, p0s, p1, theta = peer_topk(xf, sh2, sc2, peer_w_q[layer].T, sub_keys, seq)
        vt_chunks = peer_v[layer].astype(BF16).reshape(-1, PEER_CHUNK, d).transpose(0, 2, 1)
        xf = peer_dense(ht, peer_u[layer].astype(BF16), vt_chunks, p0s, p1, theta, xf, g2, seq)
    return xf.reshape(bsz, seq, d)
```

```python
import functools
import math

import jax
import jax.numpy as jnp
from jax import lax
from jax.experimental import pallas as pl
from jax.experimental.pallas import tpu as pltpu

F32 = jnp.float32
BF16 = jnp.bfloat16
HI = lax.Precision.HIGHEST
EPS = 1e-6
NEG_INF = float("-inf")

VMEM_LIMIT_BYTES = 48 * 1024 * 1024
LANES = 128

GLA_HEADS = 4
GLA_HK = 128
GLA_HV = 256
GLA_GATE_RANK = 16
GLA_GATE_NORM = 16.0
GLA_CHUNK = 64

FOX_HD = 64
FOX_HEADS = 16

PEER_HEADS = 8
PEER_NKEYS = 128
PEER_TOPK = 16
PEER_CHUNK = 2048
PEER_DENSE_VMEM_BYTES = 56 * 1024 * 1024


def _params(*sem):
    return pltpu.CompilerParams(dimension_semantics=sem, vmem_limit_bytes=VMEM_LIMIT_BYTES)


def _adaln_rows(x, shift, scale):
    ms = jnp.mean(x * x, axis=-1, keepdims=True)
    return x * lax.rsqrt(ms + EPS) * (1.0 + scale) + shift


def _mod_kernel(c_ref, w_ref, b_ref, o_ref):
    o_ref[...] = jnp.dot(c_ref[...], w_ref[...], preferred_element_type=F32, precision=HI) + b_ref[...]


def mod_matmul(c, w, b, tn=512):
    bsz, d = c.shape
    n = w.shape[1]
    return pl.pallas_call(
        _mod_kernel,
        grid=(n // tn,),
        in_specs=[
            pl.BlockSpec((bsz, d), lambda j: (0, 0)),
            pl.BlockSpec((d, tn), lambda j: (0, j)),
            pl.BlockSpec((1, tn), lambda j: (0, j)),
        ],
        out_specs=pl.BlockSpec((bsz, tn), lambda j: (0, j)),
        out_shape=jax.ShapeDtypeStruct((bsz, n), F32),
        compiler_params=_params("parallel"),
        name="mod_matmul",
    )(c, w, b.reshape(1, n))


def _adaln_mm_kernel(x_ref, sh_ref, sc_ref, w_ref, o_ref, h_ref):
    @pl.when(pl.program_id(1) == 0)
    def _():
        h_ref[...] = _adaln_rows(x_ref[...], sh_ref[0], sc_ref[0]).astype(h_ref.dtype)

    if w_ref.dtype == BF16:
        acc = jnp.dot(h_ref[...], w_ref[...], preferred_element_type=F32)
    else:
        acc = jnp.dot(h_ref[...], w_ref[...], preferred_element_type=F32, precision=HI)
    o_ref[...] = acc.astype(o_ref.dtype)


def adaln_matmul(xf, shift, scale, w, seq, tm=512, tn=512):
    t, d = xf.shape
    n = w.shape[1]
    tn = min(tn, n)
    per_seq = seq // tm
    return pl.pallas_call(
        _adaln_mm_kernel,
        grid=(t // tm, n // tn),
        in_specs=[
            pl.BlockSpec((tm, d), lambda i, j: (i, 0)),
            pl.BlockSpec((1, 1, d), lambda i, j: (i // per_seq, 0, 0)),
            pl.BlockSpec((1, 1, d), lambda i, j: (i // per_seq, 0, 0)),
            pl.BlockSpec((d, tn), lambda i, j: (0, j)),
        ],
        out_specs=pl.BlockSpec((tm, tn), lambda i, j: (i, j)),
        out_shape=jax.ShapeDtypeStruct((t, n), F32),
        scratch_shapes=[pltpu.VMEM((tm, d), w.dtype)],
        compiler_params=_params("parallel", "arbitrary"),
        name="adaln_matmul",
    )(xf, shift, scale, w)


GLA_SUB = 16


def _gla_kernel(q_ref, k_ref, v_ref, gl_ref, w2_ref, bg_ref, o_ref, st_ref, *, nchunk):
    c, sb = GLA_CHUNK, GLA_SUB
    nt = (((1,), (1,)), ((), ()))

    @pl.when(pl.program_id(2) == 0)
    def _():
        st_ref[...] = jnp.zeros_like(st_ref)

    row_c = lax.broadcasted_iota(jnp.int32, (c, GLA_HK), 0)
    row_b = lax.broadcasted_iota(jnp.int32, (sb, GLA_HK), 0)
    lane_b = lax.broadcasted_iota(jnp.int32, (sb, GLA_HK), 1)
    tri = (lax.broadcasted_iota(jnp.int32, (c, c), 1) <= lax.broadcasted_iota(jnp.int32, (c, c), 0)).astype(F32)

    def chunk(ci, carry):
        r0 = pl.multiple_of(ci * c, c)
        q = q_ref[pl.ds(r0, c), :] * (GLA_HK ** -0.5)
        k = k_ref[pl.ds(r0, c), :]
        v = v_ref[pl.ds(r0, c), :].astype(BF16)
        pre = jnp.dot(gl_ref[pl.ds(r0, c), :], w2_ref[...], preferred_element_type=F32, precision=HI) + bg_ref[...]
        gk = jax.nn.log_sigmoid(pre) * (1.0 / GLA_GATE_NORM)
        b = jnp.dot(tri, gk, preferred_element_type=F32, precision=HI)
        st = st_ref[...]
        o_inter = lax.dot_general((q * jnp.exp(b)).astype(BF16), st.astype(BF16), nt, preferred_element_type=F32)

        blocks = []
        for i in range(c // sb):
            lo = i * sb
            b_i, q_i, k_i = b[lo:lo + sb], q[lo:lo + sb], k[lo:lo + sb]
            d = jnp.zeros((sb, GLA_HK), F32)
            for j in range(sb):
                rel = jnp.where(row_b >= j, b_i - b_i[j:j + 1], NEG_INF)
                m = jnp.exp(rel) * (q_i * k_i[j:j + 1])
                d = jnp.where(lane_b == lo + j, jnp.sum(m, axis=-1, keepdims=True), d)
            att = d[:, :c]
            if i > 0:
                ref = b_i[0:1]
                qs = q_i * jnp.exp(b_i - ref)
                ks = k * jnp.exp(jnp.where(row_c < lo, ref - b, NEG_INF))
                att = att + lax.dot_general(qs.astype(BF16), ks.astype(BF16), nt, preferred_element_type=F32)
            blocks.append(att)
        attn = jnp.concatenate(blocks, axis=0)
        o_ref[pl.ds(r0, c), :] = o_inter + jnp.dot(attn.astype(BF16), v, preferred_element_type=F32)
        b_last = b[c - 1:c, :]
        kdec = (k * jnp.exp(b_last - b)).astype(BF16)
        st_ref[...] = st * jnp.exp(b_last) + lax.dot_general(
            v, kdec, (((0,), (0,)), ((), ())), preferred_element_type=F32)
        return carry

    lax.fori_loop(0, nchunk, chunk, 0)


def gla_recurrence(proj, glow, w2p, bg, bsz, seq, ts=512):
    t = proj.shape[0]
    ns = seq // ts
    kcol = (GLA_HEADS * GLA_HK) // GLA_HK
    vcol = (2 * GLA_HEADS * GLA_HK) // GLA_HV
    return pl.pallas_call(
        functools.partial(_gla_kernel, nchunk=ts // GLA_CHUNK),
        grid=(bsz, GLA_HEADS, ns),
        in_specs=[
            pl.BlockSpec((ts, GLA_HK), lambda b, h, s: (b * ns + s, h)),
            pl.BlockSpec((ts, GLA_HK), lambda b, h, s: (b * ns + s, kcol + h)),
            pl.BlockSpec((ts, GLA_HV), lambda b, h, s: (b * ns + s, vcol + h)),
            pl.BlockSpec((ts, LANES), lambda b, h, s: (b * ns + s, 0)),
            pl.BlockSpec((LANES, GLA_HK), lambda b, h, s: (0, h)),
            pl.BlockSpec((1, GLA_HK), lambda b, h, s: (0, h)),
        ],
        out_specs=pl.BlockSpec((ts, GLA_HV), lambda b, h, s: (b * ns + s, h)),
        out_shape=jax.ShapeDtypeStruct((t, GLA_HEADS * GLA_HV), F32),
        scratch_shapes=[pltpu.VMEM((GLA_HV, GLA_HK), F32)],
        compiler_params=_params("parallel", "parallel", "arbitrary"),
        name="gla_recurrence",
    )(proj, proj, proj, glow, w2p, bg)


def _gla_out_kernel(o_ref, g_ref, x_ref, g1_ref, on_ref, w_ref, out_ref):
    parts = []
    for h in range(GLA_HEADS):
        oh = o_ref[:, h * GLA_HV:(h + 1) * GLA_HV]
        ms = jnp.mean(oh * oh, axis=-1, keepdims=True)
        parts.append(oh * lax.rsqrt(ms + EPS) * on_ref[...])
    y = jnp.concatenate(parts, axis=-1) * jax.nn.silu(g_ref[...])
    mix = jnp.dot(y.astype(BF16), w_ref[...], preferred_element_type=F32)
    out_ref[...] = x_ref[...] + g1_ref[0] * mix


def _fox_out_kernel(o_ref, g_ref, x_ref, g1_ref, w_ref, out_ref):
    y = o_ref[...] * jax.nn.sigmoid(g_ref[...])
    mix = jnp.dot(y.astype(BF16), w_ref[...], preferred_element_type=F32)
    out_ref[...] = x_ref[...] + g1_ref[0] * mix


def mixer_out(o, gsrc, gcol, xf, g1, w_bf16, seq, o_norm=None, tm=512):
    t, d = xf.shape
    per_seq = seq // tm
    row = lambda i: (i, 0)
    in_specs = [
        pl.BlockSpec((tm, d), row),
        pl.BlockSpec((tm, d), lambda i: (i, gcol)),
        pl.BlockSpec((tm, d), row),
        pl.BlockSpec((1, 1, d), lambda i: (i // per_seq, 0, 0)),
    ]
    args = [o, gsrc, xf, g1]
    if o_norm is not None:
        in_specs.append(pl.BlockSpec((1, GLA_HV), lambda i: (0, 0)))
        args.append(o_norm.reshape(1, GLA_HV))
        body = _gla_out_kernel
    else:
        body = _fox_out_kernel
    in_specs.append(pl.BlockSpec((d, d), lambda i: (0, 0)))
    args.append(w_bf16)
    return pl.pallas_call(
        body,
        grid=(t // tm,),
        in_specs=in_specs,
        out_specs=pl.BlockSpec((tm, d), row),
        out_shape=jax.ShapeDtypeStruct((t, d), F32),
        compiler_params=_params("parallel"),
        name="mixer_out",
    )(*args)


FOX_BIAS_PIECES = 3
FOX_BIAS_LANES = 2 * FOX_BIAS_PIECES


def _fgate_kernel(f_ref, bf_ref, kb_ref, carry_ref):
    tc = f_ref.shape[0]

    @pl.when(pl.program_id(1) == 0)
    def _():
        carry_ref[...] = jnp.zeros_like(carry_ref)

    logf = jax.nn.log_sigmoid(f_ref[...] + bf_ref[...])
    tri = (lax.broadcasted_iota(jnp.int32, (tc, tc), 1) <= lax.broadcasted_iota(jnp.int32, (tc, tc), 0)).astype(F32)
    cum = jnp.dot(tri, logf, preferred_element_type=F32, precision=HI) + carry_ref[...]
    carry_ref[...] = cum[tc - 1:tc, :]
    hi = cum.astype(BF16).astype(F32)
    mid = (cum - hi).astype(BF16).astype(F32)
    lo = cum - hi - mid
    pos = lax.broadcasted_iota(jnp.int32, cum.shape, 1) % LANES
    piece = pos % FOX_BIAS_PIECES
    out = jnp.where(piece == 0, hi, jnp.where(piece == 1, mid, lo))
    kb_ref[...] = jnp.where(pos < FOX_BIAS_LANES, out, 0.0).astype(BF16)


def forget_cumsum(f_ext, bf_ext, bsz, seq, tc=256):
    t, n = f_ext.shape
    ns = seq // tc
    return pl.pallas_call(
        _fgate_kernel,
        grid=(bsz, ns),
        in_specs=[
            pl.BlockSpec((tc, n), lambda b, s: (b * ns + s, 0)),
            pl.BlockSpec((1, n), lambda b, s: (0, 0)),
        ],
        out_specs=pl.BlockSpec((tc, n), lambda b, s: (b * ns + s, 0)),
        out_shape=jax.ShapeDtypeStruct((t, n), BF16),
        scratch_shapes=[pltpu.VMEM((1, n), F32)],
        compiler_params=_params("parallel", "arbitrary"),
        name="forget_cumsum",
    )(f_ext, bf_ext)


def _pair_headnorm(x2, bd, gain2):
    ms = jnp.dot(x2 * x2, bd, preferred_element_type=F32, precision=HI)
    return x2 * lax.rsqrt(ms + EPS) * gain2


def _kv_prep_kernel(kv_ref, bd_ref, kn_ref, ko_ref, vt_ref):
    d = ko_ref.shape[1]
    for cb in range(d // LANES):
        cols = slice(cb * LANES, (cb + 1) * LANES)
        ko_ref[:, cols] = _pair_headnorm(kv_ref[:, cols], bd_ref[...], kn_ref[...]).astype(BF16)
        vt_ref[cols, :] = kv_ref[:, d + cb * LANES:d + (cb + 1) * LANES].T.astype(BF16)


def fox_kv_prep(kv, bd, k_gain2, tm=512):
    t, d2 = kv.shape
    d = d2 // 2
    return pl.pallas_call(
        _kv_prep_kernel,
        grid=(t // tm,),
        in_specs=[
            pl.BlockSpec((tm, d2), lambda i: (i, 0)),
            pl.BlockSpec((LANES, LANES), lambda i: (0, 0)),
            pl.BlockSpec((1, LANES), lambda i: (0, 0)),
        ],
        out_specs=[pl.BlockSpec((tm, d), lambda i: (i, 0)), pl.BlockSpec((d, tm), lambda i: (0, i))],
        out_shape=[jax.ShapeDtypeStruct((t, d), BF16), jax.ShapeDtypeStruct((d, t), BF16)],
        compiler_params=_params("parallel"),
        name="fox_kv_prep",
    )(kv, bd, k_gain2)


def _fox_attn_kernel(qi_ref, ki_ref, q_ref, k_ref, kb_ref, vt_ref, bd_ref, qn_ref, o_ref,
                     qs_ref, m_ref, l_ref, acc_ref):
    qi = qi_ref[pl.program_id(2)]
    ki = ki_ref[pl.program_id(2)]
    tq = q_ref.shape[0]
    tk = k_ref.shape[0]

    @pl.when(ki == 0)
    def _():
        qt = (_pair_headnorm(q_ref[...], bd_ref[...], qn_ref[...]) * (FOX_HD ** -0.5)).T
        row = lax.broadcasted_iota(jnp.int32, qt.shape, 0)
        for j in range(2):
            top = jnp.where((row >= j * FOX_HD) & (row < (j + 1) * FOX_HD), qt, 0.0)
            lo = j * FOX_BIAS_PIECES
            bias = jnp.where((row >= lo) & (row < lo + FOX_BIAS_PIECES), -1.0, 0.0)
            qs_ref[j] = jnp.concatenate([top, bias], axis=0).astype(BF16)
        m_ref[...] = jnp.full_like(m_ref, NEG_INF)
        l_ref[...] = jnp.zeros_like(l_ref)
        acc_ref[...] = jnp.zeros_like(acc_ref)

    def update(diagonal):
        kext = jnp.concatenate([k_ref[...], kb_ref[...]], axis=1)
        vt = vt_ref[...]
        if diagonal:
            causal = (lax.broadcasted_iota(jnp.int32, (tk, tq), 0) <= lax.broadcasted_iota(jnp.int32, (tk, tq), 1))
        for j in range(2):
            st = jnp.dot(kext, qs_ref[j], preferred_element_type=F32)
            if diagonal:
                st = jnp.where(causal, st, NEG_INF)
            m_prev = m_ref[j]
            m_new = jnp.maximum(m_prev, jnp.max(st, axis=0, keepdims=True))
            alpha = jnp.exp(m_prev - m_new)
            p = jnp.exp(st - m_new)
            l_ref[j] = alpha * l_ref[j] + jnp.sum(p, axis=0, keepdims=True)
            acc_ref[j] = alpha * acc_ref[j] + jnp.dot(vt, p.astype(BF16), preferred_element_type=F32)
            m_ref[j] = m_new

    @pl.when(ki < qi)
    def _():
        update(False)

    @pl.when(ki == qi)
    def _():
        update(True)
        row = lax.broadcasted_iota(jnp.int32, (LANES, tq), 0)
        o_ref[...] = jnp.where(row < FOX_HD, acc_ref[0] / l_ref[0], acc_ref[1] / l_ref[1]).T


def fox_attention(qg, kn, kb, vt, bd, q_gain2, bsz, seq, tq=512):
    t = qg.shape[0]
    d = FOX_HEADS * FOX_HD
    nq = seq // tq
    npair = FOX_HEADS // 2
    pairs = [(i, j) for i in range(nq) for j in range(i + 1)]
    qi_tab = jnp.asarray([p[0] for p in pairs], jnp.int32)
    ki_tab = jnp.asarray([p[1] for p in pairs], jnp.int32)
    qblk = lambda b, h, s, qt, kt: (b * nq + qt[s], h)
    kblk = lambda b, h, s, qt, kt: (b * nq + kt[s], h)
    const = lambda b, h, s, qt, kt: (0, 0)
    grid_spec = pltpu.PrefetchScalarGridSpec(
        num_scalar_prefetch=2,
        grid=(bsz, npair, len(pairs)),
        in_specs=[
            pl.BlockSpec((tq, LANES), qblk),
            pl.BlockSpec((tq, LANES), kblk),
            pl.BlockSpec((tq, LANES), kblk),
            pl.BlockSpec((LANES, tq), lambda b, h, s, qt, kt: (h, b * nq + kt[s])),
            pl.BlockSpec((LANES, LANES), const),
            pl.BlockSpec((1, LANES), const),
        ],
        out_specs=pl.BlockSpec((tq, LANES), qblk),
        scratch_shapes=[
            pltpu.VMEM((2, 2 * LANES, tq), BF16),
            pltpu.VMEM((2, 1, tq), F32),
            pltpu.VMEM((2, 1, tq), F32),
            pltpu.VMEM((2, LANES, tq), F32),
        ],
    )
    return pl.pallas_call(
        _fox_attn_kernel,
        grid_spec=grid_spec,
        out_shape=jax.ShapeDtypeStruct((t, d), F32),
        compiler_params=_params("parallel", "parallel", "arbitrary"),
        name="fox_attention",
    )(qi_tab, ki_tab, qg, kn, kb, vt, bd, q_gain2)


SUBLANES = 8


def _merge_desc(v):
    n = len(v)
    if n == 1:
        return v
    half = n // 2
    hi = [jnp.maximum(v[i], v[i + half]) for i in range(half)]
    lo = [jnp.minimum(v[i], v[i + half]) for i in range(half)]
    return _merge_desc(hi) + _merge_desc(lo)


def _sort_desc(v):
    n = len(v)
    if n == 1:
        return v
    return _merge_desc(_sort_desc(v[:n // 2]) + _sort_desc(v[n // 2:])[::-1])


def _top16_values(groups):
    lists = _sort_desc(groups)
    shift = SUBLANES // 2
    while shift >= 1:
        partner = [pltpu.roll(a, shift, 0) for a in lists]
        if len(lists) < PEER_TOPK:
            lists = _merge_desc(lists + partner[::-1])
        else:
            n = len(lists)
            lists = _merge_desc([jnp.maximum(lists[i], partner[n - 1 - i]) for i in range(n)])
        shift //= 2
    return lists


_CAND_PAIRS = [(i, j) for i in range(PEER_TOPK) for j in range(PEER_TOPK) if (i + 1) * (j + 1) <= PEER_TOPK]


def _split_bf16(x):
    hi = x.astype(BF16)
    return hi, (x - hi.astype(F32)).astype(BF16)


def _peer_topk_kernel(x_ref, sh_ref, sc_ref, wqh_ref, wql_ref, sk_ref, ht_ref, p0s_ref, p1_ref, th_ref, qt_ref):
    h = _adaln_rows(x_ref[...], sh_ref[0], sc_ref[0])
    ht = h.T
    ht_ref[...] = ht.astype(BF16)
    dq = PEER_NKEYS
    ngroups = PEER_NKEYS // SUBLANES
    tb = ht.shape[1]
    sub = lax.broadcasted_iota(jnp.int32, (SUBLANES, tb), 0)
    thetas = []
    ht_hi, ht_lo = _split_bf16(ht)
    qt_ref[...] = (jnp.dot(wqh_ref[...], ht_hi, preferred_element_type=F32)
                   + jnp.dot(wqh_ref[...], ht_lo, preferred_element_type=F32)
                   + jnp.dot(wql_ref[...], ht_hi, preferred_element_type=F32))
    for head in range(PEER_HEADS):
        groups, tops = [], []
        for half in range(2):
            hp = 2 * head + half
            st = jnp.dot(sk_ref[hp], qt_ref[hp * dq:(hp + 1) * dq, :],
                         preferred_element_type=F32, precision=HI)
            g = [st[SUBLANES * k:SUBLANES * (k + 1), :] for k in range(ngroups)]
            m = functools.reduce(jnp.maximum, g)
            for shift in (4, 2, 1):
                m = jnp.maximum(m, pltpu.roll(m, shift, 0))
            g = [jnp.exp(v - m) for v in g]
            groups.append(g)
            tops.append(_top16_values(g))
        p0, p1 = tops

        def top_products(a, b):
            packed = []
            for k in range(0, len(_CAND_PAIRS), SUBLANES):
                acc = jnp.zeros((SUBLANES, tb), F32)
                for s_, (i, j) in enumerate(_CAND_PAIRS[k:k + SUBLANES]):
                    acc = jnp.where(sub == s_, a[i] * b[j], acc)
                packed.append(acc)
            while len(packed) & (len(packed) - 1):
                packed.append(jnp.zeros((SUBLANES, tb), F32))
            return _top16_values(packed)

        inv_z = 1.0 / functools.reduce(jnp.add, top_products(p0, p1))
        p0s = [v * inv_z for v in p0]
        thetas.append(top_products(p0s, p1)[PEER_TOPK - 1][0:1, :])
        for half, ref in enumerate((p0s_ref, p1_ref)):
            t16 = tops[half][PEER_TOPK - 1]
            for k in range(ngroups):
                g = groups[half][k]
                tab = jnp.where(g >= t16, g, 0.0)
                ref[head, SUBLANES * k:SUBLANES * (k + 1), :] = tab * inv_z if half == 0 else tab
    th_ref[...] = jnp.concatenate(thetas, axis=0)


def peer_topk(xf, shift, scale, wq_t, sub_keys, seq, tb=256):
    wq_hi, wq_lo = _split_bf16(wq_t)
    t, d = xf.shape
    per_seq = seq // tb
    nh, nk = PEER_HEADS, PEER_NKEYS
    tab = jax.ShapeDtypeStruct((nh, nk, t), F32)
    tab_spec = pl.BlockSpec((nh, nk, tb), lambda i: (0, 0, i))
    return pl.pallas_call(
        _peer_topk_kernel,
        grid=(t // tb,),
        in_specs=[
            pl.BlockSpec((tb, d), lambda i: (i, 0)),
            pl.BlockSpec((1, 1, d), lambda i: (i // per_seq, 0, 0)),
            pl.BlockSpec((1, 1, d), lambda i: (i // per_seq, 0, 0)),
            pl.BlockSpec(wq_t.shape, lambda i: (0, 0)),
            pl.BlockSpec(wq_t.shape, lambda i: (0, 0)),
            pl.BlockSpec(sub_keys.shape, lambda i: (0, 0, 0)),
        ],
        out_specs=[
            pl.BlockSpec((d, tb), lambda i: (0, i)),
            tab_spec, tab_spec,
            pl.BlockSpec((nh, tb), lambda i: (0, i)),
        ],
        out_shape=[jax.ShapeDtypeStruct((d, t), BF16), tab, tab, jax.ShapeDtypeStruct((nh, t), F32)],
        scratch_shapes=[pltpu.VMEM((wq_t.shape[0], tb), F32)],
        compiler_params=_params("parallel"),
        name="peer_topk",
    )(xf, shift, scale, wq_hi, wq_lo, sub_keys)


def _peer_dense_kernel(ht_ref, u_ref, vt_ref, p0s_ref, p1_ref, th_ref, x_ref, g2_ref, out_ref,
                       at_ref, wt_ref, acc_ref, *, tiles):
    e = pl.program_id(1)
    nk = PEER_NKEYS
    slot = e % 2
    prev = 1 - slot
    nsub, _, sub = wt_ref.shape

    def u_matmul(s):
        return jnp.dot(u_ref[...], ht_ref[:, s * sub:(s + 1) * sub], preferred_element_type=F32)

    @pl.when(e == 0)
    def _():
        acc_ref[...] = jnp.zeros_like(acc_ref)
        for s in range(nsub):
            at_ref[0, s] = u_matmul(s)

    @pl.when(e > 0)
    def _():
        key_base = (e - 1) * tiles
        for s in range(nsub):
            cols = slice(s * sub, (s + 1) * sub)
            a_new = u_matmul(s)
            for i in range(tiles):
                rows = slice(i * nk, (i + 1) * nk)
                a = at_ref[prev, s, rows, :]
                act = 0.5 * a * (1.0 + lax.erf(a * (1.0 / math.sqrt(2.0))))
                g = jnp.zeros_like(a)
                for head in range(PEER_HEADS):
                    w = p0s_ref[head, pl.ds(key_base + i, 1), cols] * p1_ref[head, :, cols]
                    g = jnp.where(w >= th_ref[pl.ds(head, 1), cols], g + w, g)
                wt_ref[s, rows, :] = (g * act).astype(BF16)
            acc_ref[s] += jnp.dot(vt_ref[0], wt_ref[s], preferred_element_type=F32)
            at_ref[slot, s] = a_new

    @pl.when(e == pl.num_programs(1) - 1)
    def _():
        for s in range(nsub):
            rows = slice(s * sub, (s + 1) * sub)
            out_ref[rows, :] = x_ref[rows, :] + g2_ref[0] * acc_ref[s].T


def peer_dense(ht, u_bf16, vt_chunks, p0s, p1, theta, xf, g2, seq, tb=512, sub=256):
    t, d = xf.shape
    nchunk, _, ec = vt_chunks.shape
    per_seq = seq // tb
    nh, nk = PEER_HEADS, PEER_NKEYS
    tab_spec = pl.BlockSpec((nh, nk, tb), lambda i, e: (0, 0, i))
    return pl.pallas_call(
        functools.partial(_peer_dense_kernel, tiles=ec // nk),
        grid=(t // tb, nchunk + 1),
        in_specs=[
            pl.BlockSpec((d, tb), lambda i, e: (0, i)),
            pl.BlockSpec((ec, d), lambda i, e: (jnp.minimum(e, nchunk - 1), 0)),
            pl.BlockSpec((1, d, ec), lambda i, e: (jnp.maximum(e - 1, 0), 0, 0)),
            tab_spec, tab_spec,
            pl.BlockSpec((nh, tb), lambda i, e: (0, i)),
            pl.BlockSpec((tb, d), lambda i, e: (i, 0)),
            pl.BlockSpec((1, 1, d), lambda i, e: (i // per_seq, 0, 0)),
        ],
        out_specs=pl.BlockSpec((tb, d), lambda i, e: (i, 0)),
        out_shape=jax.ShapeDtypeStruct((t, d), F32),
        scratch_shapes=[
            pltpu.VMEM((2, tb // sub, ec, sub), F32),
            pltpu.VMEM((tb // sub, ec, sub), BF16),
            pltpu.VMEM((tb // sub, d, sub), F32),
        ],
        compiler_params=pltpu.CompilerParams(dimension_semantics=("parallel", "arbitrary"),
                                             vmem_limit_bytes=PEER_DENSE_VMEM_BYTES),
        name="peer_dense",
    )(ht, u_bf16, vt_chunks, p0s, p1, theta, xf, g2)


def _pad_cols(w, n):
    return jnp.pad(w, ((0, 0), (0, n - w.shape[1])))


def kernel(x, c, mod_w, mod_b, gla_w_in, gla_w_gate2, gla_b_gate, gla_o_norm, gla_w_out, kv_mod_w, kv_mod_b, fox_w_kvf, fox_b_f, fox_k_norm, fox_w_qg, fox_q_norm, fox_w_out, peer_w_q, peer_sub_keys, peer_u, peer_v):
    bsz, seq, d = x.shape
    depth = mod_w.shape[0]
    n_gla = gla_w_in.shape[0]
    xf = x.reshape(bsz * seq, d)
    gla_dk = GLA_HEADS * GLA_HK
    gla_main = 2 * gla_dk + 2 * GLA_HEADS * GLA_HV

    shared = None
    for layer in range(depth):
        if layer == n_gla:
            kv_mod = mod_matmul(c, kv_mod_w, kv_mod_b).reshape(bsz, 2, 1, d)
            kv_sh, kv_sc = kv_mod[:, 0], kv_mod[:, 1]
            kv = adaln_matmul(xf, kv_sh, kv_sc, fox_w_kvf[:, :2 * d].astype(BF16), seq)
            pos = jnp.arange(d) % LANES
            src = 2 * (jnp.arange(d) // LANES) + pos // FOX_BIAS_PIECES
            used = pos < FOX_BIAS_LANES
            src = jnp.where(used, src, 0)
            w_f = jnp.where(used[None, :], fox_w_kvf[:, 2 * d:][:, src], 0.0)
            b_f = jnp.where(used, fox_b_f[src], 0.0).reshape(1, d)
            f_ext = adaln_matmul(xf, kv_sh, kv_sc, w_f, seq)
            kb = forget_cumsum(f_ext, b_f, bsz, seq)
            head_avg = jnp.kron(jnp.eye(LANES // FOX_HD, dtype=F32), jnp.full((FOX_HD, FOX_HD), 1.0 / FOX_HD, F32))
            kn, vt = fox_kv_prep(kv, head_avg, jnp.tile(fox_k_norm, LANES // FOX_HD).reshape(1, LANES))
            shared = (kn, kb, vt, head_avg)

        mod = mod_matmul(c, mod_w[layer], mod_b[layer]).reshape(bsz, 6, 1, d)
        sh1, sc1, g1, sh2, sc2, g2 = (mod[:, i] for i in range(6))

        if layer < n_gla:
            w_in = gla_w_in[layer]
            proj = adaln_matmul(xf, sh1, sc1, w_in[:, :gla_main].astype(BF16), seq)
            glow = adaln_matmul(xf, sh1, sc1, _pad_cols(w_in[:, gla_main:], LANES), seq)
            w2p = jnp.pad(gla_w_gate2[layer], ((0, LANES - GLA_GATE_RANK), (0, 0)))
            o = gla_recurrence(proj, glow, w2p, gla_b_gate[layer].reshape(1, gla_dk), bsz, seq)
            xf = mixer_out(o, proj, (2 * gla_dk + GLA_HEADS * GLA_HV) // d, xf, g1,
                           gla_w_out[layer].astype(BF16), seq, o_norm=gla_o_norm[layer])
        else:
            j = layer - n_gla
            kn, kb, vt, head_avg = shared
            qg = adaln_matmul(xf, sh1, sc1, fox_w_qg[j].astype(BF16), seq)
            o = fox_attention(qg, kn, kb, vt, head_avg,
                              jnp.tile(fox_q_norm[j], LANES // FOX_HD).reshape(1, LANES), bsz, seq)
            xf = mixer_out(o, qg, 1, xf, g1, fox_w_out[j].astype(BF16), seq)

        sub_keys = peer_sub_keys[layer].reshape(2 * PEER_HEADS, PEER_NKEYS, -1)
        ht, p0s, p1, theta = peer_topk(xf, sh2, sc2, peer_w_q[layer].T, sub_keys, seq)
        vt_chunks = peer_v[layer].astype(BF16).reshape(-1, PEER_CHUNK, d).transpose(0, 2, 1)
        xf = peer_dense(ht, peer_u[layer].astype(BF16), vt_chunks, p0s, p1, theta, xf, g2, seq)
    return xf.reshape(bsz, seq, d)
```

```python
import functools
import math

import jax
import jax.numpy as jnp
from jax import lax
from jax.experimental import pallas as pl
from jax.experimental.pallas import tpu as pltpu

F32 = jnp.float32
BF16 = jnp.bfloat16
HI = lax.Precision.HIGHEST
EPS = 1e-6
NEG_INF = float("-inf")

VMEM_LIMIT_BYTES = 48 * 1024 * 1024
LANES = 128

GLA_HEADS = 4
GLA_HK = 128
GLA_HV = 256
GLA_GATE_RANK = 16
GLA_GATE_NORM = 16.0
GLA_CHUNK = 64

FOX_HD = 64
FOX_HEADS = 16

PEER_HEADS = 8
PEER_NKEYS = 128
PEER_TOPK = 16
PEER_CHUNK = 2048
PEER_DENSE_VMEM_BYTES = 56 * 1024 * 1024


def _params(*sem):
    return pltpu.CompilerParams(dimension_semantics=sem, vmem_limit_bytes=VMEM_LIMIT_BYTES)


def _adaln_rows(x, shift, scale):
    ms = jnp.mean(x * x, axis=-1, keepdims=True)
    return x * lax.rsqrt(ms + EPS) * (1.0 + scale) + shift


def _mod_kernel(c_ref, w_ref, b_ref, o_ref):
    o_ref[...] = jnp.dot(c_ref[...], w_ref[...], preferred_element_type=F32, precision=HI) + b_ref[...]


def mod_matmul(c, w, b, tn=512):
    bsz, d = c.shape
    n = w.shape[1]
    return pl.pallas_call(
        _mod_kernel,
        grid=(n // tn,),
        in_specs=[
            pl.BlockSpec((bsz, d), lambda j: (0, 0)),
            pl.BlockSpec((d, tn), lambda j: (0, j)),
            pl.BlockSpec((1, tn), lambda j: (0, j)),
        ],
        out_specs=pl.BlockSpec((bsz, tn), lambda j: (0, j)),
        out_shape=jax.ShapeDtypeStruct((bsz, n), F32),
        compiler_params=_params("parallel"),
        name="mod_matmul",
    )(c, w, b.reshape(1, n))


def _adaln_mm_kernel(x_ref, sh_ref, sc_ref, w_ref, o_ref, h_ref):
    @pl.when(pl.program_id(1) == 0)
    def _():
        h_ref[...] = _adaln_rows(x_ref[...], sh_ref[0], sc_ref[0]).astype(h_ref.dtype)

    if w_ref.dtype == BF16:
        acc = jnp.dot(h_ref[...], w_ref[...], preferred_element_type=F32)
    else:
        acc = jnp.dot(h_ref[...], w_ref[...], preferred_element_type=F32, precision=HI)
    o_ref[...] = acc.astype(o_ref.dtype)


def adaln_matmul(xf, shift, scale, w, seq, tm=512, tn=512):
    t, d = xf.shape
    n = w.shape[1]
    tn = min(tn, n)
    per_seq = seq // tm
    return pl.pallas_call(
        _adaln_mm_kernel,
        grid=(t // tm, n // tn),
        in_specs=[
            pl.BlockSpec((tm, d), lambda i, j: (i, 0)),
            pl.BlockSpec((1, 1, d), lambda i, j: (i // per_seq, 0, 0)),
            pl.BlockSpec((1, 1, d), lambda i, j: (i // per_seq, 0, 0)),
            pl.BlockSpec((d, tn), lambda i, j: (0, j)),
        ],
        out_specs=pl.BlockSpec((tm, tn), lambda i, j: (i, j)),
        out_shape=jax.ShapeDtypeStruct((t, n), F32),
        scratch_shapes=[pltpu.VMEM((tm, d), w.dtype)],
        compiler_params=_params("parallel", "arbitrary"),
        name="adaln_matmul",
    )(xf, shift, scale, w)


GLA_SUB = 16


def _gla_kernel(q_ref, k_ref, v_ref, gl_ref, w2_ref, bg_ref, o_ref, st_ref, *, nchunk):
    c, sb = GLA_CHUNK, GLA_SUB
    nt = (((1,), (1,)), ((), ()))

    @pl.when(pl.program_id(2) == 0)
    def _():
        st_ref[...] = jnp.zeros_like(st_ref)

    row_c = lax.broadcasted_iota(jnp.int32, (c, GLA_HK), 0)
    row_b = lax.broadcasted_iota(jnp.int32, (sb, GLA_HK), 0)
    lane_b = lax.broadcasted_iota(jnp.int32, (sb, GLA_HK), 1)
    tri = (lax.broadcasted_iota(jnp.int32, (c, c), 1) <= lax.broadcasted_iota(jnp.int32, (c, c), 0)).astype(F32)

    def chunk(ci, carry):
        r0 = pl.multiple_of(ci * c, c)
        q = q_ref[pl.ds(r0, c), :] * (GLA_HK ** -0.5)
        k = k_ref[pl.ds(r0, c), :]
        v = v_ref[pl.ds(r0, c), :].astype(BF16)
        pre = jnp.dot(gl_ref[pl.ds(r0, c), :], w2_ref[...], preferred_element_type=F32, precision=HI) + bg_ref[...]
        gk = jax.nn.log_sigmoid(pre) * (1.0 / GLA_GATE_NORM)
        b = jnp.dot(tri, gk, preferred_element_type=F32, precision=HI)
        st = st_ref[...]
        o_inter = lax.dot_general((q * jnp.exp(b)).astype(BF16), st.astype(BF16), nt, preferred_element_type=F32)

        blocks = []
        for i in range(c // sb):
            lo = i * sb
            b_i, q_i, k_i = b[lo:lo + sb], q[lo:lo + sb], k[lo:lo + sb]
            d = jnp.zeros((sb, GLA_HK), F32)
            for j in range(sb):
                rel = jnp.where(row_b >= j, b_i - b_i[j:j + 1], NEG_INF)
                m = jnp.exp(rel) * (q_i * k_i[j:j + 1])
                d = jnp.where(lane_b == lo + j, jnp.sum(m, axis=-1, keepdims=True), d)
            att = d[:, :c]
            if i > 0:
                ref = b_i[0:1]
                qs = q_i * jnp.exp(b_i - ref)
                ks = k * jnp.exp(jnp.where(row_c < lo, ref - b, NEG_INF))
                att = att + lax.dot_general(qs.astype(BF16), ks.astype(BF16), nt, preferred_element_type=F32)
            blocks.append(att)
        attn = jnp.concatenate(blocks, axis=0)
        o_ref[pl.ds(r0, c), :] = o_inter + jnp.dot(attn.astype(BF16), v, preferred_element_type=F32)
        b_last = b[c - 1:c, :]
        kdec = (k * jnp.exp(b_last - b)).astype(BF16)
        st_ref[...] = st * jnp.exp(b_last) + lax.dot_general(
            v, kdec, (((0,), (0,)), ((), ())), preferred_element_type=F32)
        return carry

    lax.fori_loop(0, nchunk, chunk, 0)


def gla_recurrence(proj, glow, w2p, bg, bsz, seq, ts=512):
    t = proj.shape[0]
    ns = seq // ts
    kcol = (GLA_HEADS * GLA_HK) // GLA_HK
    vcol = (2 * GLA_HEADS * GLA_HK) // GLA_HV
    return pl.pallas_call(
        functools.partial(_gla_kernel, nchunk=ts // GLA_CHUNK),
        grid=(bsz, GLA_HEADS, ns),
        in_specs=[
            pl.BlockSpec((ts, GLA_HK), lambda b, h, s: (b * ns + s, h)),
            pl.BlockSpec((ts, GLA_HK), lambda b, h, s: (b * ns + s, kcol + h)),
            pl.BlockSpec((ts, GLA_HV), lambda b, h, s: (b * ns + s, vcol + h)),
            pl.BlockSpec((ts, LANES), lambda b, h, s: (b * ns + s, 0)),
            pl.BlockSpec((LANES, GLA_HK), lambda b, h, s: (0, h)),
            pl.BlockSpec((1, GLA_HK), lambda b, h, s: (0, h)),
        ],
        out_specs=pl.BlockSpec((ts, GLA_HV), lambda b, h, s: (b * ns + s, h)),
        out_shape=jax.ShapeDtypeStruct((t, GLA_HEADS * GLA_HV), F32),
        scratch_shapes=[pltpu.VMEM((GLA_HV, GLA_HK), F32)],
        compiler_params=_params("parallel", "parallel", "arbitrary"),
        name="gla_recurrence",
    )(proj, proj, proj, glow, w2p, bg)


def _gla_out_kernel(o_ref, g_ref, x_ref, g1_ref, on_ref, w_ref, out_ref):
    parts = []
    for h in range(GLA_HEADS):
        oh = o_ref[:, h * GLA_HV:(h + 1) * GLA_HV]
        ms = jnp.mean(oh * oh, axis=-1, keepdims=True)
        parts.append(oh * lax.rsqrt(ms + EPS) * on_ref[...])
    y = jnp.concatenate(parts, axis=-1) * jax.nn.silu(g_ref[...])
    mix = jnp.dot(y.astype(BF16), w_ref[...], preferred_element_type=F32)
    out_ref[...] = x_ref[...] + g1_ref[0] * mix


def _fox_out_kernel(o_ref, g_ref, x_ref, g1_ref, w_ref, out_ref):
    y = o_ref[...] * jax.nn.sigmoid(g_ref[...])
    mix = jnp.dot(y.astype(BF16), w_ref[...], preferred_element_type=F32)
    out_ref[...] = x_ref[...] + g1_ref[0] * mix


def mixer_out(o, gsrc, gcol, xf, g1, w_bf16, seq, o_norm=None, tm=512):
    t, d = xf.shape
    per_seq = seq // tm
    row = lambda i: (i, 0)
    in_specs = [
        pl.BlockSpec((tm, d), row),
        pl.BlockSpec((tm, d), lambda i: (i, gcol)),
        pl.BlockSpec((tm, d), row),
        pl.BlockSpec((1, 1, d), lambda i: (i // per_seq, 0, 0)),
    ]
    args = [o, gsrc, xf, g1]
    if o_norm is not None:
        in_specs.append(pl.BlockSpec((1, GLA_HV), lambda i: (0, 0)))
        args.append(o_norm.reshape(1, GLA_HV))
        body = _gla_out_kernel
    else:
        body = _fox_out_kernel
    in_specs.append(pl.BlockSpec((d, d), lambda i: (0, 0)))
    args.append(w_bf16)
    return pl.pallas_call(
        body,
        grid=(t // tm,),
        in_specs=in_specs,
        out_specs=pl.BlockSpec((tm, d), row),
        out_shape=jax.ShapeDtypeStruct((t, d), F32),
        compiler_params=_params("parallel"),
        name="mixer_out",
    )(*args)


FOX_BIAS_PIECES = 3
FOX_BIAS_LANES = 2 * FOX_BIAS_PIECES


def _fgate_kernel(f_ref, bf_ref, sel_ref, kb_ref, carry_ref):
    tc = f_ref.shape[0]

    @pl.when(pl.program_id(1) == 0)
    def _():
        carry_ref[...] = jnp.zeros_like(carry_ref)

    logf = jax.nn.log_sigmoid(f_ref[...] + bf_ref[...])
    tri = (lax.broadcasted_iota(jnp.int32, (tc, tc), 1) <= lax.broadcasted_iota(jnp.int32, (tc, tc), 0)).astype(F32)
    cum_heads = jnp.dot(tri, logf, preferred_element_type=F32, precision=HI) + carry_ref[...]
    carry_ref[...] = cum_heads[tc - 1:tc, :]
    cum = jnp.dot(cum_heads, sel_ref[...], preferred_element_type=F32, precision=HI)
    hi = cum.astype(BF16).astype(F32)
    mid = (cum - hi).astype(BF16).astype(F32)
    lo = cum - hi - mid
    pos = lax.broadcasted_iota(jnp.int32, cum.shape, 1) % LANES
    piece = pos % FOX_BIAS_PIECES
    out = jnp.where(piece == 0, hi, jnp.where(piece == 1, mid, lo))
    kb_ref[...] = jnp.where(pos < FOX_BIAS_LANES, out, 0.0).astype(BF16)


def forget_cumsum(f, bf_pad, sel, bsz, seq, tc=256):
    t, nf = f.shape
    n = sel.shape[1]
    ns = seq // tc
    return pl.pallas_call(
        _fgate_kernel,
        grid=(bsz, ns),
        in_specs=[
            pl.BlockSpec((tc, nf), lambda b, s: (b * ns + s, 0)),
            pl.BlockSpec((1, nf), lambda b, s: (0, 0)),
            pl.BlockSpec((nf, n), lambda b, s: (0, 0)),
        ],
        out_specs=pl.BlockSpec((tc, n), lambda b, s: (b * ns + s, 0)),
        out_shape=jax.ShapeDtypeStruct((t, n), BF16),
        scratch_shapes=[pltpu.VMEM((1, nf), F32)],
        compiler_params=_params("parallel", "arbitrary"),
        name="forget_cumsum",
    )(f, bf_pad, sel)


def _pair_headnorm(x2, bd, gain2):
    ms = jnp.dot(x2 * x2, bd, preferred_element_type=F32, precision=HI)
    return x2 * lax.rsqrt(ms + EPS) * gain2


def _kv_prep_kernel(kv_ref, bd_ref, kn_ref, ko_ref, vt_ref):
    d = ko_ref.shape[1]
    for cb in range(d // LANES):
        cols = slice(cb * LANES, (cb + 1) * LANES)
        ko_ref[:, cols] = _pair_headnorm(kv_ref[:, cols], bd_ref[...], kn_ref[...]).astype(BF16)
        vt_ref[cols, :] = kv_ref[:, d + cb * LANES:d + (cb + 1) * LANES].T.astype(BF16)


def fox_kv_prep(kv, bd, k_gain2, tm=512):
    t, d2 = kv.shape
    d = d2 // 2
    return pl.pallas_call(
        _kv_prep_kernel,
        grid=(t // tm,),
        in_specs=[
            pl.BlockSpec((tm, d2), lambda i: (i, 0)),
            pl.BlockSpec((LANES, LANES), lambda i: (0, 0)),
            pl.BlockSpec((1, LANES), lambda i: (0, 0)),
        ],
        out_specs=[pl.BlockSpec((tm, d), lambda i: (i, 0)), pl.BlockSpec((d, tm), lambda i: (0, i))],
        out_shape=[jax.ShapeDtypeStruct((t, d), BF16), jax.ShapeDtypeStruct((d, t), BF16)],
        compiler_params=_params("parallel"),
        name="fox_kv_prep",
    )(kv, bd, k_gain2)


def _fox_attn_kernel(qi_ref, ki_ref, q_ref, k_ref, kb_ref, vt_ref, bd_ref, qn_ref, o_ref,
                     qs_ref, m_ref, l_ref, acc_ref):
    qi = qi_ref[pl.program_id(2)]
    ki = ki_ref[pl.program_id(2)]
    tq = q_ref.shape[0]
    tk = k_ref.shape[0]
    npb = q_ref.shape[1] // LANES

    @pl.when(ki == 0)
    def _():
        for pr in range(npb):
            cols = slice(pr * LANES, (pr + 1) * LANES)
            qt = (_pair_headnorm(q_ref[:, cols], bd_ref[...], qn_ref[...]) * (FOX_HD ** -0.5)).T
            row = lax.broadcasted_iota(jnp.int32, qt.shape, 0)
            for j in range(2):
                top = jnp.where((row >= j * FOX_HD) & (row < (j + 1) * FOX_HD), qt, 0.0)
                lo = j * FOX_BIAS_PIECES
                bias = jnp.where((row >= lo) & (row < lo + FOX_BIAS_PIECES), -1.0, 0.0)
                qs_ref[2 * pr + j] = jnp.concatenate([top, bias], axis=0).astype(BF16)
        m_ref[...] = jnp.full_like(m_ref, NEG_INF)
        l_ref[...] = jnp.zeros_like(l_ref)
        acc_ref[...] = jnp.zeros_like(acc_ref)

    def update(diagonal):
        if diagonal:
            causal = (lax.broadcasted_iota(jnp.int32, (tk, tq), 0) <= lax.broadcasted_iota(jnp.int32, (tk, tq), 1))
        for pr in range(npb):
            cols = slice(pr * LANES, (pr + 1) * LANES)
            kext = jnp.concatenate([k_ref[:, cols], kb_ref[:, cols]], axis=1)
            vt = vt_ref[cols, :]
            for j in range(2):
                h = 2 * pr + j
                st = jnp.dot(kext, qs_ref[h], preferred_element_type=F32)
                if diagonal:
                    st = jnp.where(causal, st, NEG_INF)
                m_prev = m_ref[h]
                m_new = jnp.maximum(m_prev, jnp.max(st, axis=0, keepdims=True))
                alpha = jnp.exp(m_prev - m_new)
                p = jnp.exp(st - m_new)
                l_ref[h] = alpha * l_ref[h] + jnp.sum(p, axis=0, keepdims=True)
                acc_ref[h] = alpha * acc_ref[h] + jnp.dot(vt, p.astype(BF16), preferred_element_type=F32)
                m_ref[h] = m_new

    @pl.when(ki < qi)
    def _():
        update(False)

    @pl.when(ki == qi)
    def _():
        update(True)
        row = lax.broadcasted_iota(jnp.int32, (LANES, tq), 0)
        for pr in range(npb):
            h0, h1 = 2 * pr, 2 * pr + 1
            o_ref[:, pr * LANES:(pr + 1) * LANES] = jnp.where(
                row < FOX_HD, acc_ref[h0] / l_ref[h0], acc_ref[h1] / l_ref[h1]).T


def fox_attention(qg, kn, kb, vt, bd, q_gain2, bsz, seq, tq=512, heads_per_step=4):
    t = qg.shape[0]
    d = FOX_HEADS * FOX_HD
    nq = seq // tq
    hs = heads_per_step
    wcol = hs * FOX_HD
    pairs = [(i, j) for i in range(nq) for j in range(i + 1)]
    qi_tab = jnp.asarray([p[0] for p in pairs], jnp.int32)
    ki_tab = jnp.asarray([p[1] for p in pairs], jnp.int32)
    qblk = lambda b, h, s, qt, kt: (b * nq + qt[s], h)
    kblk = lambda b, h, s, qt, kt: (b * nq + kt[s], h)
    const = lambda b, h, s, qt, kt: (0, 0)
    grid_spec = pltpu.PrefetchScalarGridSpec(
        num_scalar_prefetch=2,
        grid=(bsz, FOX_HEADS // hs, len(pairs)),
        in_specs=[
            pl.BlockSpec((tq, wcol), qblk),
            pl.BlockSpec((tq, wcol), kblk),
            pl.BlockSpec((tq, wcol), kblk),
            pl.BlockSpec((wcol, tq), lambda b, h, s, qt, kt: (h, b * nq + kt[s])),
            pl.BlockSpec((LANES, LANES), const),
            pl.BlockSpec((1, LANES), const),
        ],
        out_specs=pl.BlockSpec((tq, wcol), qblk),
        scratch_shapes=[
            pltpu.VMEM((hs, 2 * LANES, tq), BF16),
            pltpu.VMEM((hs, 1, tq), F32),
            pltpu.VMEM((hs, 1, tq), F32),
            pltpu.VMEM((hs, LANES, tq), F32),
        ],
    )
    return pl.pallas_call(
        _fox_attn_kernel,
        grid_spec=grid_spec,
        out_shape=jax.ShapeDtypeStruct((t, d), F32),
        compiler_params=_params("parallel", "parallel", "arbitrary"),
        name="fox_attention",
    )(qi_tab, ki_tab, qg, kn, kb, vt, bd, q_gain2)


SUBLANES = 8


def _merge_desc(v):
    n = len(v)
    if n == 1:
        return v
    half = n // 2
    hi = [jnp.maximum(v[i], v[i + half]) for i in range(half)]
    lo = [jnp.minimum(v[i], v[i + half]) for i in range(half)]
    return _merge_desc(hi) + _merge_desc(lo)


def _sort_desc(v):
    n = len(v)
    if n == 1:
        return v
    return _merge_desc(_sort_desc(v[:n // 2]) + _sort_desc(v[n // 2:])[::-1])


def _top16_values(groups):
    lists = _sort_desc(groups)
    shift = SUBLANES // 2
    while shift >= 1:
        partner = [pltpu.roll(a, shift, 0) for a in lists]
        if len(lists) < PEER_TOPK:
            lists = _merge_desc(lists + partner[::-1])
        else:
            n = len(lists)
            lists = _merge_desc([jnp.maximum(lists[i], partner[n - 1 - i]) for i in range(n)])
        shift //= 2
    return lists


_CAND_PAIRS = [(i, j) for i in range(PEER_TOPK) for j in range(PEER_TOPK) if (i + 1) * (j + 1) <= PEER_TOPK]


def _split_bf16(x):
    hi = x.astype(BF16)
    return hi, (x - hi.astype(F32)).astype(BF16)


def _peer_topk_kernel(x_ref, sh_ref, sc_ref, wqh_ref, wql_ref, sk_ref, ht_ref, p0s_ref, p1_ref, th_ref, qt_ref):
    h = _adaln_rows(x_ref[...], sh_ref[0], sc_ref[0])
    ht = h.T
    ht_ref[...] = ht.astype(BF16)
    dq = PEER_NKEYS
    ngroups = PEER_NKEYS // SUBLANES
    tb = ht.shape[1]
    sub = lax.broadcasted_iota(jnp.int32, (SUBLANES, tb), 0)
    thetas = []
    ht_hi, ht_lo = _split_bf16(ht)
    qt_ref[...] = (jnp.dot(wqh_ref[...], ht_hi, preferred_element_type=F32)
                   + jnp.dot(wqh_ref[...], ht_lo, preferred_element_type=F32)
                   + jnp.dot(wql_ref[...], ht_hi, preferred_element_type=F32))
    for head in range(PEER_HEADS):
        groups, tops = [], []
        for half in range(2):
            hp = 2 * head + half
            st = jnp.dot(sk_ref[hp], qt_ref[hp * dq:(hp + 1) * dq, :],
                         preferred_element_type=F32, precision=HI)
            g = [st[SUBLANES * k:SUBLANES * (k + 1), :] for k in range(ngroups)]
            m = functools.reduce(jnp.maximum, g)
            for shift in (4, 2, 1):
                m = jnp.maximum(m, pltpu.roll(m, shift, 0))
            g = [jnp.exp(v - m) for v in g]
            groups.append(g)
            tops.append(_top16_values(g))
        p0, p1 = tops

        def top_products(a, b):
            packed = []
            for k in range(0, len(_CAND_PAIRS), SUBLANES):
                acc = jnp.zeros((SUBLANES, tb), F32)
                for s_, (i, j) in enumerate(_CAND_PAIRS[k:k + SUBLANES]):
                    acc = jnp.where(sub == s_, a[i] * b[j], acc)
                packed.append(acc)
            while len(packed) & (len(packed) - 1):
                packed.append(jnp.zeros((SUBLANES, tb), F32))
            return _top16_values(packed)

        inv_z = 1.0 / functools.reduce(jnp.add, top_products(p0, p1))
        p0s = [v * inv_z for v in p0]
        thetas.append(top_products(p0s, p1)[PEER_TOPK - 1][0:1, :])
        for half, ref in enumerate((p0s_ref, p1_ref)):
            t16 = tops[half][PEER_TOPK - 1]
            for k in range(ngroups):
                g = groups[half][k]
                tab = jnp.where(g >= t16, g, 0.0)
                ref[head, SUBLANES * k:SUBLANES * (k + 1), :] = tab * inv_z if half == 0 else tab
    th_ref[...] = jnp.concatenate(thetas, axis=0)


def peer_topk(xf, shift, scale, wq_t, sub_keys, seq, tb=256):
    wq_hi, wq_lo = _split_bf16(wq_t)
    t, d = xf.shape
    per_seq = seq // tb
    nh, nk = PEER_HEADS, PEER_NKEYS
    tab = jax.ShapeDtypeStruct((nh, nk, t), F32)
    tab_spec = pl.BlockSpec((nh, nk, tb), lambda i: (0, 0, i))
    return pl.pallas_call(
        _peer_topk_kernel,
        grid=(t // tb,),
        in_specs=[
            pl.BlockSpec((tb, d), lambda i: (i, 0)),
            pl.BlockSpec((1, 1, d), lambda i: (i // per_seq, 0, 0)),
            pl.BlockSpec((1, 1, d), lambda i: (i // per_seq, 0, 0)),
            pl.BlockSpec(wq_t.shape, lambda i: (0, 0)),
            pl.BlockSpec(wq_t.shape, lambda i: (0, 0)),
            pl.BlockSpec(sub_keys.shape, lambda i: (0, 0, 0)),
        ],
        out_specs=[
            pl.BlockSpec((d, tb), lambda i: (0, i)),
            tab_spec, tab_spec,
            pl.BlockSpec((nh, tb), lambda i: (0, i)),
        ],
        out_shape=[jax.ShapeDtypeStruct((d, t), BF16), tab, tab, jax.ShapeDtypeStruct((nh, t), F32)],
        scratch_shapes=[pltpu.VMEM((wq_t.shape[0], tb), F32)],
        compiler_params=_params("parallel"),
        name="peer_topk",
    )(xf, shift, scale, wq_hi, wq_lo, sub_keys)


def _peer_dense_kernel(ht_ref, u_ref, vt_ref, p0s_ref, p1_ref, th_ref, x_ref, g2_ref, out_ref,
                       at_ref, wt_ref, acc_ref, *, tiles):
    e = pl.program_id(1)
    nk = PEER_NKEYS
    slot = e % 2
    prev = 1 - slot
    nsub, _, sub = wt_ref.shape

    def u_matmul(s):
        return jnp.dot(u_ref[...], ht_ref[:, s * sub:(s + 1) * sub], preferred_element_type=F32)

    @pl.when(e == 0)
    def _():
        acc_ref[...] = jnp.zeros_like(acc_ref)
        for s in range(nsub):
            at_ref[0, s] = u_matmul(s)

    @pl.when(e > 0)
    def _():
        key_base = (e - 1) * tiles
        for s in range(nsub):
            cols = slice(s * sub, (s + 1) * sub)
            a_new = u_matmul(s)
            for i in range(tiles):
                rows = slice(i * nk, (i + 1) * nk)
                a = at_ref[prev, s, rows, :]
                act = 0.5 * a * (1.0 + lax.erf(a * (1.0 / math.sqrt(2.0))))
                g = jnp.zeros_like(a)
                for head in range(PEER_HEADS):
                    w = p0s_ref[head, pl.ds(key_base + i, 1), cols] * p1_ref[head, :, cols]
                    g = jnp.where(w >= th_ref[pl.ds(head, 1), cols], g + w, g)
                wt_ref[s, rows, :] = (g * act).astype(BF16)
            acc_ref[s] += jnp.dot(vt_ref[0], wt_ref[s], preferred_element_type=F32)
            at_ref[slot, s] = a_new

    @pl.when(e == pl.num_programs(1) - 1)
    def _():
        for s in range(nsub):
            rows = slice(s * sub, (s + 1) * sub)
            out_ref[rows, :] = x_ref[rows, :] + g2_ref[0] * acc_ref[s].T


def peer_dense(ht, u_bf16, vt_chunks, p0s, p1, theta, xf, g2, seq, tb=512, sub=256):
    t, d = xf.shape
    nchunk, _, ec = vt_chunks.shape
    per_seq = seq // tb
    nh, nk = PEER_HEADS, PEER_NKEYS
    tab_spec = pl.BlockSpec((nh, nk, tb), lambda i, e: (0, 0, i))
    return pl.pallas_call(
        functools.partial(_peer_dense_kernel, tiles=ec // nk),
        grid=(t // tb, nchunk + 1),
        in_specs=[
            pl.BlockSpec((d, tb), lambda i, e: (0, i)),
            pl.BlockSpec((ec, d), lambda i, e: (jnp.minimum(e, nchunk - 1), 0)),
            pl.BlockSpec((1, d, ec), lambda i, e: (jnp.maximum(e - 1, 0), 0, 0)),
            tab_spec, tab_spec,
            pl.BlockSpec((nh, tb), lambda i, e: (0, i)),
            pl.BlockSpec((tb, d), lambda i, e: (i, 0)),
            pl.BlockSpec((1, 1, d), lambda i, e: (i // per_seq, 0, 0)),
        ],
        out_specs=pl.BlockSpec((tb, d), lambda i, e: (i, 0)),
        out_shape=jax.ShapeDtypeStruct((t, d), F32),
        scratch_shapes=[
            pltpu.VMEM((2, tb // sub, ec, sub), F32),
            pltpu.VMEM((tb // sub, ec, sub), BF16),
            pltpu.VMEM((tb // sub, d, sub), F32),
        ],
        compiler_params=pltpu.CompilerParams(dimension_semantics=("parallel", "arbitrary"),
                                             vmem_limit_bytes=PEER_DENSE_VMEM_BYTES),
        name="peer_dense",
    )(ht, u_bf16, vt_chunks, p0s, p1, theta, xf, g2)


def _pad_cols(w, n):
    return jnp.pad(w, ((0, 0), (0, n - w.shape[1])))


def kernel(x, c, mod_w, mod_b, gla_w_in, gla_w_gate2, gla_b_gate, gla_o_norm, gla_w_out, kv_mod_w, kv_mod_b, fox_w_kvf, fox_b_f, fox_k_norm, fox_w_qg, fox_q_norm, fox_w_out, peer_w_q, peer_sub_keys, peer_u, peer_v):
    bsz, seq, d = x.shape
    depth = mod_w.shape[0]
    n_gla = gla_w_in.shape[0]
    xf = x.reshape(bsz * seq, d)
    gla_dk = GLA_HEADS * GLA_HK
    gla_main = 2 * gla_dk + 2 * GLA_HEADS * GLA_HV

    shared = None
    for layer in range(depth):
        if layer == n_gla:
            kv_mod = mod_matmul(c, kv_mod_w, kv_mod_b).reshape(bsz, 2, 1, d)
            kv_sh, kv_sc = kv_mod[:, 0], kv_mod[:, 1]
            kv = adaln_matmul(xf, kv_sh, kv_sc, fox_w_kvf[:, :2 * d].astype(BF16), seq)
            pos = jnp.arange(d) % LANES
            src = 2 * (jnp.arange(d) // LANES) + pos // FOX_BIAS_PIECES
            used = pos < FOX_BIAS_LANES
            sel = ((src[None, :] == jnp.arange(LANES)[:, None]) & used[None, :]).astype(F32)
            f = adaln_matmul(xf, kv_sh, kv_sc, _pad_cols(fox_w_kvf[:, 2 * d:], LANES), seq)
            bf_pad = jnp.pad(fox_b_f, (0, LANES - FOX_HEADS)).reshape(1, LANES)
            kb = forget_cumsum(f, bf_pad, sel, bsz, seq)
            head_avg = jnp.kron(jnp.eye(LANES // FOX_HD, dtype=F32), jnp.full((FOX_HD, FOX_HD), 1.0 / FOX_HD, F32))
            kn, vt = fox_kv_prep(kv, head_avg, jnp.tile(fox_k_norm, LANES // FOX_HD).reshape(1, LANES))
            shared = (kn, kb, vt, head_avg)

        mod = mod_matmul(c, mod_w[layer], mod_b[layer]).reshape(bsz, 6, 1, d)
        sh1, sc1, g1, sh2, sc2, g2 = (mod[:, i] for i in range(6))

        if layer < n_gla:
            w_in = gla_w_in[layer]
            proj = adaln_matmul(xf, sh1, sc1, w_in[:, :gla_main].astype(BF16), seq)
            glow = adaln_matmul(xf, sh1, sc1, _pad_cols(w_in[:, gla_main:], LANES), seq)
            w2p = jnp.pad(gla_w_gate2[layer], ((0, LANES - GLA_GATE_RANK), (0, 0)))
            o = gla_recurrence(proj, glow, w2p, gla_b_gate[layer].reshape(1, gla_dk), bsz, seq)
            xf = mixer_out(o, proj, (2 * gla_dk + GLA_HEADS * GLA_HV) // d, xf, g1,
                           gla_w_out[layer].astype(BF16), seq, o_norm=gla_o_norm[layer])
        else:
            j = layer - n_gla
            kn, kb, vt, head_avg = shared
            qg = adaln_matmul(xf, sh1, sc1, fox_w_qg[j].astype(BF16), seq)
            o = fox_attention(qg, kn, kb, vt, head_avg,
                              jnp.tile(fox_q_norm[j], LANES // FOX_HD).reshape(1, LANES), bsz, seq)
            xf = mixer_out(o, qg, 1, xf, g1, fox_w_out[j].astype(BF16), seq)

        sub_keys = peer_sub_keys[layer].reshape(2 * PEER_HEADS, PEER_NKEYS, -1)
        ht, p0s, p1, theta = peer_topk(xf, sh2, sc2, peer_w_q[layer].T, sub_keys, seq)
        vt_chunks = peer_v[layer].astype(BF16).reshape(-1, PEER_CHUNK, d).transpose(0, 2, 1)
        xf = peer_dense(ht, peer_u[layer].astype(BF16), vt_chunks, p0s, p1, theta, xf, g2, seq)
    return xf.reshape(bsz, seq, d)
```

```python
import functools
import math

import jax
import jax.numpy as jnp
from jax import lax
from jax.experimental import pallas as pl
from jax.experimental.pallas import tpu as pltpu

F32 = jnp.float32
BF16 = jnp.bfloat16
HI = lax.Precision.HIGHEST
EPS = 1e-6
NEG_INF = float("-inf")

VMEM_LIMIT_BYTES = 48 * 1024 * 1024
LANES = 128

GLA_HEADS = 4
GLA_HK = 128
GLA_HV = 256
GLA_GATE_RANK = 16
GLA_GATE_NORM = 16.0
GLA_CHUNK = 64

FOX_HD = 64
FOX_HEADS = 16

PEER_HEADS = 8
PEER_NKEYS = 128
PEER_TOPK = 16
PEER_CHUNK = 2048
PEER_DENSE_VMEM_BYTES = 56 * 1024 * 1024


def _params(*sem):
    return pltpu.CompilerParams(dimension_semantics=sem, vmem_limit_bytes=VMEM_LIMIT_BYTES)


def _adaln_rows(x, shift, scale):
    ms = jnp.mean(x * x, axis=-1, keepdims=True)
    return x * lax.rsqrt(ms + EPS) * (1.0 + scale) + shift


def _mod_kernel(c_ref, w_ref, b_ref, o_ref):
    o_ref[...] = jnp.dot(c_ref[...], w_ref[...], preferred_element_type=F32, precision=HI) + b_ref[...]


def mod_matmul(c, w, b, tn=512):
    bsz, d = c.shape
    n = w.shape[1]
    return pl.pallas_call(
        _mod_kernel,
        grid=(n // tn,),
        in_specs=[
            pl.BlockSpec((bsz, d), lambda j: (0, 0)),
            pl.BlockSpec((d, tn), lambda j: (0, j)),
            pl.BlockSpec((1, tn), lambda j: (0, j)),
        ],
        out_specs=pl.BlockSpec((bsz, tn), lambda j: (0, j)),
        out_shape=jax.ShapeDtypeStruct((bsz, n), F32),
        compiler_params=_params("parallel"),
        name="mod_matmul",
    )(c, w, b.reshape(1, n))


def _adaln_mm_kernel(x_ref, sh_ref, sc_ref, w_ref, o_ref, h_ref):
    @pl.when(pl.program_id(1) == 0)
    def _():
        h_ref[...] = _adaln_rows(x_ref[...], sh_ref[0], sc_ref[0]).astype(h_ref.dtype)

    if w_ref.dtype == BF16:
        acc = jnp.dot(h_ref[...], w_ref[...], preferred_element_type=F32)
    else:
        acc = jnp.dot(h_ref[...], w_ref[...], preferred_element_type=F32, precision=HI)
    o_ref[...] = acc.astype(o_ref.dtype)


def adaln_matmul(xf, shift, scale, w, seq, tm=512, tn=512):
    t, d = xf.shape
    n = w.shape[1]
    tn = min(tn, n)
    per_seq = seq // tm
    return pl.pallas_call(
        _adaln_mm_kernel,
        grid=(t // tm, n // tn),
        in_specs=[
            pl.BlockSpec((tm, d), lambda i, j: (i, 0)),
            pl.BlockSpec((1, 1, d), lambda i, j: (i // per_seq, 0, 0)),
            pl.BlockSpec((1, 1, d), lambda i, j: (i // per_seq, 0, 0)),
            pl.BlockSpec((d, tn), lambda i, j: (0, j)),
        ],
        out_specs=pl.BlockSpec((tm, tn), lambda i, j: (i, j)),
        out_shape=jax.ShapeDtypeStruct((t, n), F32),
        scratch_shapes=[pltpu.VMEM((tm, d), w.dtype)],
        compiler_params=_params("parallel", "arbitrary"),
        name="adaln_matmul",
    )(xf, shift, scale, w)


GLA_SUB = 16


def _gla_kernel(q_ref, k_ref, v_ref, gl_ref, w2_ref, bg_ref, o_ref, st_ref, *, nchunk):
    c, sb = GLA_CHUNK, GLA_SUB
    nt = (((1,), (1,)), ((), ()))

    @pl.when(pl.program_id(2) == 0)
    def _():
        st_ref[...] = jnp.zeros_like(st_ref)

    row_c = lax.broadcasted_iota(jnp.int32, (c, GLA_HK), 0)
    row_b = lax.broadcasted_iota(jnp.int32, (sb, GLA_HK), 0)
    lane_b = lax.broadcasted_iota(jnp.int32, (sb, GLA_HK), 1)
    tri = (lax.broadcasted_iota(jnp.int32, (c, c), 1) <= lax.broadcasted_iota(jnp.int32, (c, c), 0)).astype(F32)

    def chunk(ci, carry):
        r0 = pl.multiple_of(ci * c, c)
        q = q_ref[pl.ds(r0, c), :] * (GLA_HK ** -0.5)
        k = k_ref[pl.ds(r0, c), :]
        v = v_ref[pl.ds(r0, c), :].astype(BF16)
        pre = jnp.dot(gl_ref[pl.ds(r0, c), :], w2_ref[...], preferred_element_type=F32, precision=HI) + bg_ref[...]
        gk = jax.nn.log_sigmoid(pre) * (1.0 / GLA_GATE_NORM)
        b = jnp.dot(tri, gk, preferred_element_type=F32, precision=HI)
        st = st_ref[...]
        o_inter = lax.dot_general((q * jnp.exp(b)).astype(BF16), st.astype(BF16), nt, preferred_element_type=F32)

        blocks = []
        for i in range(c // sb):
            lo = i * sb
            b_i, q_i, k_i = b[lo:lo + sb], q[lo:lo + sb], k[lo:lo + sb]
            d = jnp.zeros((sb, GLA_HK), F32)
            for j in range(sb):
                rel = jnp.where(row_b >= j, b_i - b_i[j:j + 1], NEG_INF)
                m = jnp.exp(rel) * (q_i * k_i[j:j + 1])
                d = jnp.where(lane_b == lo + j, jnp.sum(m, axis=-1, keepdims=True), d)
            att = d[:, :c]
            if i > 0:
                ref = b_i[0:1]
                qs = q_i * jnp.exp(b_i - ref)
                ks = k * jnp.exp(jnp.where(row_c < lo, ref - b, NEG_INF))
                att = att + lax.dot_general(qs.astype(BF16), ks.astype(BF16), nt, preferred_element_type=F32)
            blocks.append(att)
        attn = jnp.concatenate(blocks, axis=0)
        o_ref[pl.ds(r0, c), :] = o_inter + jnp.dot(attn.astype(BF16), v, preferred_element_type=F32)
        b_last = b[c - 1:c, :]
        kdec = (k * jnp.exp(b_last - b)).astype(BF16)
        st_ref[...] = st * jnp.exp(b_last) + lax.dot_general(
            v, kdec, (((0,), (0,)), ((), ())), preferred_element_type=F32)
        return carry

    lax.fori_loop(0, nchunk, chunk, 0)


def gla_recurrence(proj, glow, w2p, bg, bsz, seq, ts=512):
    t = proj.shape[0]
    ns = seq // ts
    kcol = (GLA_HEADS * GLA_HK) // GLA_HK
    vcol = (2 * GLA_HEADS * GLA_HK) // GLA_HV
    return pl.pallas_call(
        functools.partial(_gla_kernel, nchunk=ts // GLA_CHUNK),
        grid=(bsz, GLA_HEADS, ns),
        in_specs=[
            pl.BlockSpec((ts, GLA_HK), lambda b, h, s: (b * ns + s, h)),
            pl.BlockSpec((ts, GLA_HK), lambda b, h, s: (b * ns + s, kcol + h)),
            pl.BlockSpec((ts, GLA_HV), lambda b, h, s: (b * ns + s, vcol + h)),
            pl.BlockSpec((ts, LANES), lambda b, h, s: (b * ns + s, 0)),
            pl.BlockSpec((LANES, GLA_HK), lambda b, h, s: (0, h)),
            pl.BlockSpec((1, GLA_HK), lambda b, h, s: (0, h)),
        ],
        out_specs=pl.BlockSpec((ts, GLA_HV), lambda b, h, s: (b * ns + s, h)),
        out_shape=jax.ShapeDtypeStruct((t, GLA_HEADS * GLA_HV), F32),
        scratch_shapes=[pltpu.VMEM((GLA_HV, GLA_HK), F32)],
        compiler_params=_params("parallel", "parallel", "arbitrary"),
        name="gla_recurrence",
    )(proj, proj, proj, glow, w2p, bg)


def _gla_out_kernel(o_ref, g_ref, x_ref, g1_ref, on_ref, w_ref, out_ref):
    parts = []
    for h in range(GLA_HEADS):
        oh = o_ref[:, h * GLA_HV:(h + 1) * GLA_HV]
        ms = jnp.mean(oh * oh, axis=-1, keepdims=True)
        parts.append(oh * lax.rsqrt(ms + EPS) * on_ref[...])
    y = jnp.concatenate(parts, axis=-1) * jax.nn.silu(g_ref[...])
    mix = jnp.dot(y.astype(BF16), w_ref[...], preferred_element_type=F32)
    out_ref[...] = x_ref[...] + g1_ref[0] * mix


def _fox_out_kernel(o_ref, g_ref, x_ref, g1_ref, w_ref, out_ref):
    y = o_ref[...] * jax.nn.sigmoid(g_ref[...])
    mix = jnp.dot(y.astype(BF16), w_ref[...], preferred_element_type=F32)
    out_ref[...] = x_ref[...] + g1_ref[0] * mix


def mixer_out(o, gsrc, gcol, xf, g1, w_bf16, seq, o_norm=None, tm=512):
    t, d = xf.shape
    per_seq = seq // tm
    row = lambda i: (i, 0)
    in_specs = [
        pl.BlockSpec((tm, d), row),
        pl.BlockSpec((tm, d), lambda i: (i, gcol)),
        pl.BlockSpec((tm, d), row),
        pl.BlockSpec((1, 1, d), lambda i: (i // per_seq, 0, 0)),
    ]
    args = [o, gsrc, xf, g1]
    if o_norm is not None:
        in_specs.append(pl.BlockSpec((1, GLA_HV), lambda i: (0, 0)))
        args.append(o_norm.reshape(1, GLA_HV))
        body = _gla_out_kernel
    else:
        body = _fox_out_kernel
    in_specs.append(pl.BlockSpec((d, d), lambda i: (0, 0)))
    args.append(w_bf16)
    return pl.pallas_call(
        body,
        grid=(t // tm,),
        in_specs=in_specs,
        out_specs=pl.BlockSpec((tm, d), row),
        out_shape=jax.ShapeDtypeStruct((t, d), F32),
        compiler_params=_params("parallel"),
        name="mixer_out",
    )(*args)


FOX_BIAS_PIECES = 3
FOX_BIAS_LANES = 2 * FOX_BIAS_PIECES


def _fgate_kernel(f_ref, bf_ref, sel_ref, kb_ref, carry_ref):
    tc = f_ref.shape[0]

    @pl.when(pl.program_id(1) == 0)
    def _():
        carry_ref[...] = jnp.zeros_like(carry_ref)

    logf = jax.nn.log_sigmoid(f_ref[...] + bf_ref[...])
    tri = (lax.broadcasted_iota(jnp.int32, (tc, tc), 1) <= lax.broadcasted_iota(jnp.int32, (tc, tc), 0)).astype(F32)
    cum_heads = jnp.dot(tri, logf, preferred_element_type=F32, precision=HI) + carry_ref[...]
    carry_ref[...] = cum_heads[tc - 1:tc, :]
    cum = jnp.dot(cum_heads, sel_ref[...], preferred_element_type=F32, precision=HI)
    hi = cum.astype(BF16).astype(F32)
    mid = (cum - hi).astype(BF16).astype(F32)
    lo = cum - hi - mid
    pos = lax.broadcasted_iota(jnp.int32, cum.shape, 1) % LANES
    piece = pos % FOX_BIAS_PIECES
    out = jnp.where(piece == 0, hi, jnp.where(piece == 1, mid, lo))
    kb_ref[...] = jnp.where(pos < FOX_BIAS_LANES, out, 0.0).astype(BF16)


def forget_cumsum(f, bf_pad, sel, bsz, seq, tc=256):
    t, nf = f.shape
    n = sel.shape[1]
    ns = seq // tc
    return pl.pallas_call(
        _fgate_kernel,
        grid=(bsz, ns),
        in_specs=[
            pl.BlockSpec((tc, nf), lambda b, s: (b * ns + s, 0)),
            pl.BlockSpec((1, nf), lambda b, s: (0, 0)),
            pl.BlockSpec((nf, n), lambda b, s: (0, 0)),
        ],
        out_specs=pl.BlockSpec((tc, n), lambda b, s: (b * ns + s, 0)),
        out_shape=jax.ShapeDtypeStruct((t, n), BF16),
        scratch_shapes=[pltpu.VMEM((1, nf), F32)],
        compiler_params=_params("parallel", "arbitrary"),
        name="forget_cumsum",
    )(f, bf_pad, sel)


def _pair_headnorm(x2, bd, gain2):
    ms = jnp.dot(x2 * x2, bd, preferred_element_type=F32, precision=HI)
    return x2 * lax.rsqrt(ms + EPS) * gain2


def _kv_prep_kernel(kv_ref, bd_ref, kn_ref, ko_ref, vt_ref):
    d = ko_ref.shape[1]
    for cb in range(d // LANES):
        cols = slice(cb * LANES, (cb + 1) * LANES)
        ko_ref[:, cols] = _pair_headnorm(kv_ref[:, cols], bd_ref[...], kn_ref[...]).astype(BF16)
        vt_ref[cols, :] = kv_ref[:, d + cb * LANES:d + (cb + 1) * LANES].T.astype(BF16)


def fox_kv_prep(kv, bd, k_gain2, tm=512):
    t, d2 = kv.shape
    d = d2 // 2
    return pl.pallas_call(
        _kv_prep_kernel,
        grid=(t // tm,),
        in_specs=[
            pl.BlockSpec((tm, d2), lambda i: (i, 0)),
            pl.BlockSpec((LANES, LANES), lambda i: (0, 0)),
            pl.BlockSpec((1, LANES), lambda i: (0, 0)),
        ],
        out_specs=[pl.BlockSpec((tm, d), lambda i: (i, 0)), pl.BlockSpec((d, tm), lambda i: (0, i))],
        out_shape=[jax.ShapeDtypeStruct((t, d), BF16), jax.ShapeDtypeStruct((d, t), BF16)],
        compiler_params=_params("parallel"),
        name="fox_kv_prep",
    )(kv, bd, k_gain2)


def _fox_attn_kernel(qi_ref, ki_ref, q_ref, k_ref, kb_ref, vt_ref, bd_ref, qn_ref, o_ref,
                     qs_ref, m_ref, l_ref, acc_ref):
    qi = qi_ref[pl.program_id(2)]
    ki = ki_ref[pl.program_id(2)]
    tq = q_ref.shape[0]
    tk = k_ref.shape[0]
    npb = q_ref.shape[1] // LANES

    @pl.when(ki == 0)
    def _():
        for pr in range(npb):
            cols = slice(pr * LANES, (pr + 1) * LANES)
            qt = (_pair_headnorm(q_ref[:, cols], bd_ref[...], qn_ref[...]) * (FOX_HD ** -0.5)).T
            row = lax.broadcasted_iota(jnp.int32, qt.shape, 0)
            for j in range(2):
                top = jnp.where((row >= j * FOX_HD) & (row < (j + 1) * FOX_HD), qt, 0.0)
                lo = j * FOX_BIAS_PIECES
                bias = jnp.where((row >= lo) & (row < lo + FOX_BIAS_PIECES), -1.0, 0.0)
                qs_ref[2 * pr + j] = jnp.concatenate([top, bias], axis=0).astype(BF16)
        m_ref[...] = jnp.full_like(m_ref, NEG_INF)
        l_ref[...] = jnp.zeros_like(l_ref)
        acc_ref[...] = jnp.zeros_like(acc_ref)

    def update(diagonal):
        if diagonal:
            causal = (lax.broadcasted_iota(jnp.int32, (tk, tq), 0) <= lax.broadcasted_iota(jnp.int32, (tk, tq), 1))
        for pr in range(npb):
            cols = slice(pr * LANES, (pr + 1) * LANES)
            kext = jnp.concatenate([k_ref[:, cols], kb_ref[:, cols]], axis=1)
            vt = vt_ref[cols, :]
            for j in range(2):
                h = 2 * pr + j
                st = jnp.dot(kext, qs_ref[h], preferred_element_type=F32)
                if diagonal:
                    st = jnp.where(causal, st, NEG_INF)
                m_prev = m_ref[h]
                m_new = jnp.maximum(m_prev, jnp.max(st, axis=0, keepdims=True))
                alpha = jnp.exp(m_prev - m_new)
                p = jnp.exp(st - m_new)
                l_ref[h] = alpha * l_ref[h] + jnp.sum(p, axis=0, keepdims=True)
                acc_ref[h] = alpha * acc_ref[h] + jnp.dot(vt, p.astype(BF16), preferred_element_type=F32)
                m_ref[h] = m_new

    @pl.when(ki < qi)
    def _():
        update(False)

    @pl.when(ki == qi)
    def _():
        update(True)
        row = lax.broadcasted_iota(jnp.int32, (LANES, tq), 0)
        for pr in range(npb):
            h0, h1 = 2 * pr, 2 * pr + 1
            o_ref[:, pr * LANES:(pr + 1) * LANES] = jnp.where(
                row < FOX_HD, acc_ref[h0] / l_ref[h0], acc_ref[h1] / l_ref[h1]).T


def fox_attention(qg, kn, kb, vt, bd, q_gain2, bsz, seq, tq=512, heads_per_step=4):
    t = qg.shape[0]
    d = FOX_HEADS * FOX_HD
    nq = seq // tq
    hs = heads_per_step
    wcol = hs * FOX_HD
    pairs = [(i, j) for i in range(nq) for j in range(i + 1)]
    qi_tab = jnp.asarray([p[0] for p in pairs], jnp.int32)
    ki_tab = jnp.asarray([p[1] for p in pairs], jnp.int32)
    qblk = lambda b, h, s, qt, kt: (b * nq + qt[s], h)
    kblk = lambda b, h, s, qt, kt: (b * nq + kt[s], h)
    const = lambda b, h, s, qt, kt: (0, 0)
    grid_spec = pltpu.PrefetchScalarGridSpec(
        num_scalar_prefetch=2,
        grid=(bsz, FOX_HEADS // hs, len(pairs)),
        in_specs=[
            pl.BlockSpec((tq, wcol), qblk),
            pl.BlockSpec((tq, wcol), kblk),
            pl.BlockSpec((tq, wcol), kblk),
            pl.BlockSpec((wcol, tq), lambda b, h, s, qt, kt: (h, b * nq + kt[s])),
            pl.BlockSpec((LANES, LANES), const),
            pl.BlockSpec((1, LANES), const),
        ],
        out_specs=pl.BlockSpec((tq, wcol), qblk),
        scratch_shapes=[
            pltpu.VMEM((hs, 2 * LANES, tq), BF16),
            pltpu.VMEM((hs, 1, tq), F32),
            pltpu.VMEM((hs, 1, tq), F32),
            pltpu.VMEM((hs, LANES, tq), F32),
        ],
    )
    return pl.pallas_call(
        _fox_attn_kernel,
        grid_spec=grid_spec,
        out_shape=jax.ShapeDtypeStruct((t, d), F32),
        compiler_params=_params("parallel", "parallel", "arbitrary"),
        name="fox_attention",
    )(qi_tab, ki_tab, qg, kn, kb, vt, bd, q_gain2)


SUBLANES = 8


def _merge_desc(v):
    n = len(v)
    if n == 1:
        return v
    half = n // 2
    hi = [jnp.maximum(v[i], v[i + half]) for i in range(half)]
    lo = [jnp.minimum(v[i], v[i + half]) for i in range(half)]
    return _merge_desc(hi) + _merge_desc(lo)


def _sort_desc(v):
    n = len(v)
    if n == 1:
        return v
    return _merge_desc(_sort_desc(v[:n // 2]) + _sort_desc(v[n // 2:])[::-1])


def _top16_values(groups):
    lists = _sort_desc(groups)
    shift = SUBLANES // 2
    while shift >= 1:
        partner = [pltpu.roll(a, shift, 0) for a in lists]
        if len(lists) < PEER_TOPK:
            lists = _merge_desc(lists + partner[::-1])
        else:
            n = len(lists)
            lists = _merge_desc([jnp.maximum(lists[i], partner[n - 1 - i]) for i in range(n)])
        shift //= 2
    return lists


_CAND_PAIRS = [(i, j) for i in range(PEER_TOPK) for j in range(PEER_TOPK) if (i + 1) * (j + 1) <= PEER_TOPK]


def _split_bf16(x):
    hi = x.astype(BF16)
    return hi, (x - hi.astype(F32)).astype(BF16)


def _peer_topk_kernel(x_ref, sh_ref, sc_ref, wqh_ref, wql_ref, sk_ref, ht_ref, p0s_ref, p1_ref, th_ref, qt_ref):
    h = _adaln_rows(x_ref[...], sh_ref[0], sc_ref[0])
    ht = h.T
    ht_ref[...] = ht.astype(BF16)
    dq = PEER_NKEYS
    ngroups = PEER_NKEYS // SUBLANES
    tb = ht.shape[1]
    sub = lax.broadcasted_iota(jnp.int32, (SUBLANES, tb), 0)
    thetas = []
    ht_hi, ht_lo = _split_bf16(ht)
    qt_ref[...] = (jnp.dot(wqh_ref[...], ht_hi, preferred_element_type=F32)
                   + jnp.dot(wqh_ref[...], ht_lo, preferred_element_type=F32)
                   + jnp.dot(wql_ref[...], ht_hi, preferred_element_type=F32))
    for head in range(PEER_HEADS):
        groups, tops = [], []
        for half in range(2):
            hp = 2 * head + half
            st = jnp.dot(sk_ref[hp], qt_ref[hp * dq:(hp + 1) * dq, :],
                         preferred_element_type=F32, precision=HI)
            g = [st[SUBLANES * k:SUBLANES * (k + 1), :] for k in range(ngroups)]
            m = functools.reduce(jnp.maximum, g)
            for shift in (4, 2, 1):
                m = jnp.maximum(m, pltpu.roll(m, shift, 0))
            g = [jnp.exp(v - m) for v in g]
            groups.append(g)
            tops.append(_top16_values(g))
        p0, p1 = tops

        def top_products(a, b):
            packed = []
            for k in range(0, len(_CAND_PAIRS), SUBLANES):
                acc = jnp.zeros((SUBLANES, tb), F32)
                for s_, (i, j) in enumerate(_CAND_PAIRS[k:k + SUBLANES]):
                    acc = jnp.where(sub == s_, a[i] * b[j], acc)
                packed.append(acc)
            while len(packed) & (len(packed) - 1):
                packed.append(jnp.zeros((SUBLANES, tb), F32))
            return _top16_values(packed)

        inv_z = 1.0 / functools.reduce(jnp.add, top_products(p0, p1))
        p0s = [v * inv_z for v in p0]
        thetas.append(top_products(p0s, p1)[PEER_TOPK - 1][0:1, :])
        for half, ref in enumerate((p0s_ref, p1_ref)):
            t16 = tops[half][PEER_TOPK - 1]
            for k in range(ngroups):
                g = groups[half][k]
                tab = jnp.where(g >= t16, g, 0.0)
                ref[head, SUBLANES * k:SUBLANES * (k + 1), :] = tab * inv_z if half == 0 else tab
    th_ref[...] = jnp.concatenate(thetas, axis=0)


def peer_topk(xf, shift, scale, wq_t, sub_keys, seq, tb=256):
    wq_hi, wq_lo = _split_bf16(wq_t)
    t, d = xf.shape
    per_seq = seq // tb
    nh, nk = PEER_HEADS, PEER_NKEYS
    tab = jax.ShapeDtypeStruct((nh, nk, t), F32)
    tab_spec = pl.BlockSpec((nh, nk, tb), lambda i: (0, 0, i))
    return pl.pallas_call(
        _peer_topk_kernel,
        grid=(t // tb,),
        in_specs=[
            pl.BlockSpec((tb, d), lambda i: (i, 0)),
            pl.BlockSpec((1, 1, d), lambda i: (i // per_seq, 0, 0)),
            pl.BlockSpec((1, 1, d), lambda i: (i // per_seq, 0, 0)),
            pl.BlockSpec(wq_t.shape, lambda i: (0, 0)),
            pl.BlockSpec(wq_t.shape, lambda i: (0, 0)),
            pl.BlockSpec(sub_keys.shape, lambda i: (0, 0, 0)),
        ],
        out_specs=[
            pl.BlockSpec((d, tb), lambda i: (0, i)),
            tab_spec, tab_spec,
            pl.BlockSpec((nh, tb), lambda i: (0, i)),
        ],
        out_shape=[jax.ShapeDtypeStruct((d, t), BF16), tab, tab, jax.ShapeDtypeStruct((nh, t), F32)],
        scratch_shapes=[pltpu.VMEM((wq_t.shape[0], tb), F32)],
        compiler_params=_params("parallel"),
        name="peer_topk",
    )(xf, shift, scale, wq_hi, wq_lo, sub_keys)


def _peer_dense_kernel(ht_ref, u_ref, vt_ref, p0s_ref, p1_ref, th_ref, x_ref, g2_ref, out_ref,
                       at_ref, wt_ref, acc_ref, *, tiles):
    e = pl.program_id(1)
    nk = PEER_NKEYS
    slot = e % 2
    prev = 1 - slot
    nsub, _, sub = wt_ref.shape

    def u_matmul(s):
        return jnp.dot(u_ref[...], ht_ref[:, s * sub:(s + 1) * sub], preferred_element_type=F32)

    @pl.when(e == 0)
    def _():
        acc_ref[...] = jnp.zeros_like(acc_ref)
        for s in range(nsub):
            at_ref[0, s] = u_matmul(s)

    @pl.when(e > 0)
    def _():
        for s in range(nsub):
            cols = slice(s * sub, (s + 1) * sub)
            a_new = u_matmul(s)
            for i in range(tiles):
                rows = slice(i * nk, (i + 1) * nk)
                a = at_ref[prev, s, rows, :]
                act = 0.5 * a * (1.0 + lax.erf(a * (1.0 / math.sqrt(2.0))))
                g = None
                for head in range(PEER_HEADS):
                    w = p0s_ref[head, i:i + 1, cols] * p1_ref[head, :, cols]
                    sel = w >= th_ref[head:head + 1, cols]
                    g = jnp.where(sel, w, 0.0) if g is None else jnp.where(sel, g + w, g)
                wt_ref[s, rows, :] = (g * act).astype(BF16)
            acc_ref[s] += jnp.dot(vt_ref[0], wt_ref[s], preferred_element_type=F32)
            at_ref[slot, s] = a_new

    @pl.when(e == pl.num_programs(1) - 1)
    def _():
        for s in range(nsub):
            rows = slice(s * sub, (s + 1) * sub)
            out_ref[rows, :] = x_ref[rows, :] + g2_ref[0] * acc_ref[s].T


def peer_dense(ht, u_bf16, vt_chunks, p0s, p1, theta, xf, g2, seq, tb=512, sub=256):
    t, d = xf.shape
    nchunk, _, ec = vt_chunks.shape
    per_seq = seq // tb
    nh, nk = PEER_HEADS, PEER_NKEYS
    tiles = ec // nk
    return pl.pallas_call(
        functools.partial(_peer_dense_kernel, tiles=tiles),
        grid=(t // tb, nchunk + 1),
        in_specs=[
            pl.BlockSpec((d, tb), lambda i, e: (0, i)),
            pl.BlockSpec((ec, d), lambda i, e: (jnp.minimum(e, nchunk - 1), 0)),
            pl.BlockSpec((1, d, ec), lambda i, e: (jnp.maximum(e - 1, 0), 0, 0)),
            pl.BlockSpec((nh, tiles, tb), lambda i, e: (0, jnp.maximum(e - 1, 0), i)),
            pl.BlockSpec((nh, nk, tb), lambda i, e: (0, 0, i)),
            pl.BlockSpec((nh, tb), lambda i, e: (0, i)),
            pl.BlockSpec((tb, d), lambda i, e: (i, 0)),
            pl.BlockSpec((1, 1, d), lambda i, e: (i // per_seq, 0, 0)),
        ],
        out_specs=pl.BlockSpec((tb, d), lambda i, e: (i, 0)),
        out_shape=jax.ShapeDtypeStruct((t, d), F32),
        scratch_shapes=[
            pltpu.VMEM((2, tb // sub, ec, sub), F32),
            pltpu.VMEM((tb // sub, ec, sub), BF16),
            pltpu.VMEM((tb // sub, d, sub), F32),
        ],
        compiler_params=pltpu.CompilerParams(dimension_semantics=("parallel", "arbitrary"),
                                             vmem_limit_bytes=PEER_DENSE_VMEM_BYTES),
        name="peer_dense",
    )(ht, u_bf16, vt_chunks, p0s, p1, theta, xf, g2)


def _pad_cols(w, n):
    return jnp.pad(w, ((0, 0), (0, n - w.shape[1])))


def kernel(x, c, mod_w, mod_b, gla_w_in, gla_w_gate2, gla_b_gate, gla_o_norm, gla_w_out, kv_mod_w, kv_mod_b, fox_w_kvf, fox_b_f, fox_k_norm, fox_w_qg, fox_q_norm, fox_w_out, peer_w_q, peer_sub_keys, peer_u, peer_v):
    bsz, seq, d = x.shape
    depth = mod_w.shape[0]
    n_gla = gla_w_in.shape[0]
    xf = x.reshape(bsz * seq, d)
    gla_dk = GLA_HEADS * GLA_HK
    gla_main = 2 * gla_dk + 2 * GLA_HEADS * GLA_HV

    shared = None
    for layer in range(depth):
        if layer == n_gla:
            kv_mod = mod_matmul(c, kv_mod_w, kv_mod_b).reshape(bsz, 2, 1, d)
            kv_sh, kv_sc = kv_mod[:, 0], kv_mod[:, 1]
            kv = adaln_matmul(xf, kv_sh, kv_sc, fox_w_kvf[:, :2 * d].astype(BF16), seq)
            pos = jnp.arange(d) % LANES
            src = 2 * (jnp.arange(d) // LANES) + pos // FOX_BIAS_PIECES
            used = pos < FOX_BIAS_LANES
            sel = ((src[None, :] == jnp.arange(LANES)[:, None]) & used[None, :]).astype(F32)
            f = adaln_matmul(xf, kv_sh, kv_sc, _pad_cols(fox_w_kvf[:, 2 * d:], LANES), seq)
            bf_pad = jnp.pad(fox_b_f, (0, LANES - FOX_HEADS)).reshape(1, LANES)
            kb = forget_cumsum(f, bf_pad, sel, bsz, seq)
            head_avg = jnp.kron(jnp.eye(LANES // FOX_HD, dtype=F32), jnp.full((FOX_HD, FOX_HD), 1.0 / FOX_HD, F32))
            kn, vt = fox_kv_prep(kv, head_avg, jnp.tile(fox_k_norm, LANES // FOX_HD).reshape(1, LANES))
            shared = (kn, kb, vt, head_avg)

        mod = mod_matmul(c, mod_w[layer], mod_b[layer]).reshape(bsz, 6, 1, d)
        sh1, sc1, g1, sh2, sc2, g2 = (mod[:, i] for i in range(6))

        if layer < n_gla:
            w_in = gla_w_in[layer]
            proj = adaln_matmul(xf, sh1, sc1, w_in[:, :gla_main].astype(BF16), seq)
            glow = adaln_matmul(xf, sh1, sc1, _pad_cols(w_in[:, gla_main:], LANES), seq)
            w2p = jnp.pad(gla_w_gate2[layer], ((0, LANES - GLA_GATE_RANK), (0, 0)))
            o = gla_recurrence(proj, glow, w2p, gla_b_gate[layer].reshape(1, gla_dk), bsz, seq)
            xf = mixer_out(o, proj, (2 * gla_dk + GLA_HEADS * GLA_HV) // d, xf, g1,
                           gla_w_out[layer].astype(BF16), seq, o_norm=gla_o_norm[layer])
        else:
            j = layer - n_gla
            kn, kb, vt, head_avg = shared
            qg = adaln_matmul(xf, sh1, sc1, fox_w_qg[j].astype(BF16), seq)
            o = fox_attention(qg, kn, kb, vt, head_avg,
                              jnp.tile(fox_q_norm[j], LANES // FOX_HD).reshape(1, LANES), bsz, seq)
            xf = mixer_out(o, qg, 1, xf, g1, fox_w_out[j].astype(BF16), seq)

        sub_keys = peer_sub_keys[layer].reshape(2 * PEER_HEADS, PEER_NKEYS, -1)
        ht, p0s, p1, theta = peer_topk(xf, sh2, sc2, peer_w_q[layer].T, sub_keys, seq)
        vt_chunks = peer_v[layer].astype(BF16).reshape(-1, PEER_CHUNK, d).transpose(0, 2, 1)
        xf = peer_dense(ht, peer_u[layer].astype(BF16), vt_chunks, p0s, p1, theta, xf, g2, seq)
    return xf.reshape(bsz, seq, d)
```

```python
import functools
import math

import jax
import jax.numpy as jnp
from jax import lax
from jax.experimental import pallas as pl
from jax.experimental.pallas import tpu as pltpu

F32 = jnp.float32
BF16 = jnp.bfloat16
HI = lax.Precision.HIGHEST
EPS = 1e-6
NEG_INF = float("-inf")

VMEM_LIMIT_BYTES = 48 * 1024 * 1024
LANES = 128

GLA_HEADS = 4
GLA_HK = 128
GLA_HV = 256
GLA_GATE_RANK = 16
GLA_GATE_NORM = 16.0
GLA_CHUNK = 64

FOX_HD = 64
FOX_HEADS = 16

PEER_HEADS = 8
PEER_NKEYS = 128
PEER_TOPK = 16
PEER_CHUNK = 2048
PEER_DENSE_VMEM_BYTES = 56 * 1024 * 1024


def _params(*sem):
    return pltpu.CompilerParams(dimension_semantics=sem, vmem_limit_bytes=VMEM_LIMIT_BYTES)


def _adaln_rows(x, shift, scale):
    ms = jnp.mean(x * x, axis=-1, keepdims=True)
    return x * lax.rsqrt(ms + EPS) * (1.0 + scale) + shift


def _mod_kernel(c_ref, w_ref, b_ref, o_ref):
    o_ref[...] = jnp.dot(c_ref[...], w_ref[...], preferred_element_type=F32, precision=HI) + b_ref[...]


def mod_matmul(c, w, b, tn=512):
    bsz, d = c.shape
    n = w.shape[1]
    return pl.pallas_call(
        _mod_kernel,
        grid=(n // tn,),
        in_specs=[
            pl.BlockSpec((bsz, d), lambda j: (0, 0)),
            pl.BlockSpec((d, tn), lambda j: (0, j)),
            pl.BlockSpec((1, tn), lambda j: (0, j)),
        ],
        out_specs=pl.BlockSpec((bsz, tn), lambda j: (0, j)),
        out_shape=jax.ShapeDtypeStruct((bsz, n), F32),
        compiler_params=_params("parallel"),
        name="mod_matmul",
    )(c, w, b.reshape(1, n))


def _adaln_mm_kernel(x_ref, sh_ref, sc_ref, w_ref, o_ref, h_ref):
    @pl.when(pl.program_id(1) == 0)
    def _():
        h_ref[...] = _adaln_rows(x_ref[...], sh_ref[0], sc_ref[0]).astype(h_ref.dtype)

    if w_ref.dtype == BF16:
        acc = jnp.dot(h_ref[...], w_ref[...], preferred_element_type=F32)
    else:
        acc = jnp.dot(h_ref[...], w_ref[...], preferred_element_type=F32, precision=HI)
    o_ref[...] = acc.astype(o_ref.dtype)


def adaln_matmul(xf, shift, scale, w, seq, tm=512, tn=512):
    t, d = xf.shape
    n = w.shape[1]
    tn = min(tn, n)
    per_seq = seq // tm
    return pl.pallas_call(
        _adaln_mm_kernel,
        grid=(t // tm, n // tn),
        in_specs=[
            pl.BlockSpec((tm, d), lambda i, j: (i, 0)),
            pl.BlockSpec((1, 1, d), lambda i, j: (i // per_seq, 0, 0)),
            pl.BlockSpec((1, 1, d), lambda i, j: (i // per_seq, 0, 0)),
            pl.BlockSpec((d, tn), lambda i, j: (0, j)),
        ],
        out_specs=pl.BlockSpec((tm, tn), lambda i, j: (i, j)),
        out_shape=jax.ShapeDtypeStruct((t, n), F32),
        scratch_shapes=[pltpu.VMEM((tm, d), w.dtype)],
        compiler_params=_params("parallel", "arbitrary"),
        name="adaln_matmul",
    )(xf, shift, scale, w)


GLA_SUB = 16


def _gla_kernel(q_ref, k_ref, v_ref, gl_ref, w2_ref, bg_ref, o_ref, st_ref, *, nchunk):
    c, sb = GLA_CHUNK, GLA_SUB
    nt = (((1,), (1,)), ((), ()))

    @pl.when(pl.program_id(2) == 0)
    def _():
        st_ref[...] = jnp.zeros_like(st_ref)

    row_c = lax.broadcasted_iota(jnp.int32, (c, GLA_HK), 0)
    row_b = lax.broadcasted_iota(jnp.int32, (sb, GLA_HK), 0)
    lane_b = lax.broadcasted_iota(jnp.int32, (sb, GLA_HK), 1)
    tri = (lax.broadcasted_iota(jnp.int32, (c, c), 1) <= lax.broadcasted_iota(jnp.int32, (c, c), 0)).astype(F32)

    def head_chunk(r0, hh):
        kc = slice(hh * GLA_HK, (hh + 1) * GLA_HK)
        vc = slice(hh * GLA_HV, (hh + 1) * GLA_HV)
        q = q_ref[pl.ds(r0, c), kc] * (GLA_HK ** -0.5)
        k = k_ref[pl.ds(r0, c), kc]
        v = v_ref[pl.ds(r0, c), vc].astype(BF16)
        pre = jnp.dot(gl_ref[pl.ds(r0, c), :], w2_ref[:, kc], preferred_element_type=F32, precision=HI) + bg_ref[:, kc]
        gk = jax.nn.log_sigmoid(pre) * (1.0 / GLA_GATE_NORM)
        b = jnp.dot(tri, gk, preferred_element_type=F32, precision=HI)
        st = st_ref[hh]
        o_inter = lax.dot_general((q * jnp.exp(b)).astype(BF16), st.astype(BF16), nt, preferred_element_type=F32)

        blocks = []
        for i in range(c // sb):
            lo = i * sb
            b_i, q_i, k_i = b[lo:lo + sb], q[lo:lo + sb], k[lo:lo + sb]
            d = jnp.zeros((sb, GLA_HK), F32)
            for j in range(sb):
                rel = jnp.where(row_b >= j, b_i - b_i[j:j + 1], NEG_INF)
                m = jnp.exp(rel) * (q_i * k_i[j:j + 1])
                d = jnp.where(lane_b == lo + j, jnp.sum(m, axis=-1, keepdims=True), d)
            att = d[:, :c]
            if i > 0:
                ref = b_i[0:1]
                qs = q_i * jnp.exp(b_i - ref)
                ks = k * jnp.exp(jnp.where(row_c < lo, ref - b, NEG_INF))
                att = att + lax.dot_general(qs.astype(BF16), ks.astype(BF16), nt, preferred_element_type=F32)
            blocks.append(att)
        attn = jnp.concatenate(blocks, axis=0)
        o_ref[pl.ds(r0, c), vc] = o_inter + jnp.dot(attn.astype(BF16), v, preferred_element_type=F32)
        b_last = b[c - 1:c, :]
        kdec = (k * jnp.exp(b_last - b)).astype(BF16)
        st_ref[hh] = st * jnp.exp(b_last) + lax.dot_general(
            v, kdec, (((0,), (0,)), ((), ())), preferred_element_type=F32)

    def chunk(ci, carry):
        r0 = pl.multiple_of(ci * c, c)
        for hh in range(st_ref.shape[0]):
            head_chunk(r0, hh)
        return carry

    lax.fori_loop(0, nchunk, chunk, 0)


def gla_recurrence(proj, glow, w2p, bg, bsz, seq, ts=512, heads_per_step=2):
    t = proj.shape[0]
    ns = seq // ts
    hs = heads_per_step
    wk, wv = hs * GLA_HK, hs * GLA_HV
    kcol = (GLA_HEADS * GLA_HK) // wk
    vcol = (2 * GLA_HEADS * GLA_HK) // wv
    return pl.pallas_call(
        functools.partial(_gla_kernel, nchunk=ts // GLA_CHUNK),
        grid=(bsz, GLA_HEADS // hs, ns),
        in_specs=[
            pl.BlockSpec((ts, wk), lambda b, h, s: (b * ns + s, h)),
            pl.BlockSpec((ts, wk), lambda b, h, s: (b * ns + s, kcol + h)),
            pl.BlockSpec((ts, wv), lambda b, h, s: (b * ns + s, vcol + h)),
            pl.BlockSpec((ts, LANES), lambda b, h, s: (b * ns + s, 0)),
            pl.BlockSpec((LANES, wk), lambda b, h, s: (0, h)),
            pl.BlockSpec((1, wk), lambda b, h, s: (0, h)),
        ],
        out_specs=pl.BlockSpec((ts, wv), lambda b, h, s: (b * ns + s, h)),
        out_shape=jax.ShapeDtypeStruct((t, GLA_HEADS * GLA_HV), F32),
        scratch_shapes=[pltpu.VMEM((hs, GLA_HV, GLA_HK), F32)],
        compiler_params=_params("parallel", "parallel", "arbitrary"),
        name="gla_recurrence",
    )(proj, proj, proj, glow, w2p, bg)


def _gla_out_kernel(o_ref, g_ref, x_ref, g1_ref, on_ref, w_ref, out_ref):
    parts = []
    for h in range(GLA_HEADS):
        oh = o_ref[:, h * GLA_HV:(h + 1) * GLA_HV]
        ms = jnp.mean(oh * oh, axis=-1, keepdims=True)
        parts.append(oh * lax.rsqrt(ms + EPS) * on_ref[...])
    y = jnp.concatenate(parts, axis=-1) * jax.nn.silu(g_ref[...])
    mix = jnp.dot(y.astype(BF16), w_ref[...], preferred_element_type=F32)
    out_ref[...] = x_ref[...] + g1_ref[0] * mix


def _fox_out_kernel(o_ref, g_ref, x_ref, g1_ref, w_ref, out_ref):
    y = o_ref[...] * jax.nn.sigmoid(g_ref[...])
    mix = jnp.dot(y.astype(BF16), w_ref[...], preferred_element_type=F32)
    out_ref[...] = x_ref[...] + g1_ref[0] * mix


def mixer_out(o, gsrc, gcol, xf, g1, w_bf16, seq, o_norm=None, tm=512):
    t, d = xf.shape
    per_seq = seq // tm
    row = lambda i: (i, 0)
    in_specs = [
        pl.BlockSpec((tm, d), row),
        pl.BlockSpec((tm, d), lambda i: (i, gcol)),
        pl.BlockSpec((tm, d), row),
        pl.BlockSpec((1, 1, d), lambda i: (i // per_seq, 0, 0)),
    ]
    args = [o, gsrc, xf, g1]
    if o_norm is not None:
        in_specs.append(pl.BlockSpec((1, GLA_HV), lambda i: (0, 0)))
        args.append(o_norm.reshape(1, GLA_HV))
        body = _gla_out_kernel
    else:
        body = _fox_out_kernel
    in_specs.append(pl.BlockSpec((d, d), lambda i: (0, 0)))
    args.append(w_bf16)
    return pl.pallas_call(
        body,
        grid=(t // tm,),
        in_specs=in_specs,
        out_specs=pl.BlockSpec((tm, d), row),
        out_shape=jax.ShapeDtypeStruct((t, d), F32),
        compiler_params=_params("parallel"),
        name="mixer_out",
    )(*args)


FOX_BIAS_PIECES = 3
FOX_BIAS_LANES = 2 * FOX_BIAS_PIECES


def _fgate_kernel(f_ref, bf_ref, sel_ref, kb_ref, carry_ref):
    tc = f_ref.shape[0]

    @pl.when(pl.program_id(1) == 0)
    def _():
        carry_ref[...] = jnp.zeros_like(carry_ref)

    logf = jax.nn.log_sigmoid(f_ref[...] + bf_ref[...])
    tri = (lax.broadcasted_iota(jnp.int32, (tc, tc), 1) <= lax.broadcasted_iota(jnp.int32, (tc, tc), 0)).astype(F32)
    cum_heads = jnp.dot(tri, logf, preferred_element_type=F32, precision=HI) + carry_ref[...]
    carry_ref[...] = cum_heads[tc - 1:tc, :]
    cum = jnp.dot(cum_heads, sel_ref[...], preferred_element_type=F32, precision=HI)
    hi = cum.astype(BF16).astype(F32)
    mid = (cum - hi).astype(BF16).astype(F32)
    lo = cum - hi - mid
    pos = lax.broadcasted_iota(jnp.int32, cum.shape, 1) % LANES
    piece = pos % FOX_BIAS_PIECES
    out = jnp.where(piece == 0, hi, jnp.where(piece == 1, mid, lo))
    kb_ref[...] = jnp.where(pos < FOX_BIAS_LANES, out, 0.0).astype(BF16)


def forget_cumsum(f, bf_pad, sel, bsz, seq, tc=256):
    t, nf = f.shape
    n = sel.shape[1]
    ns = seq // tc
    return pl.pallas_call(
        _fgate_kernel,
        grid=(bsz, ns),
        in_specs=[
            pl.BlockSpec((tc, nf), lambda b, s: (b * ns + s, 0)),
            pl.BlockSpec((1, nf), lambda b, s: (0, 0)),
            pl.BlockSpec((nf, n), lambda b, s: (0, 0)),
        ],
        out_specs=pl.BlockSpec((tc, n), lambda b, s: (b * ns + s, 0)),
        out_shape=jax.ShapeDtypeStruct((t, n), BF16),
        scratch_shapes=[pltpu.VMEM((1, nf), F32)],
        compiler_params=_params("parallel", "arbitrary"),
        name="forget_cumsum",
    )(f, bf_pad, sel)


def _pair_headnorm(x2, bd, gain2):
    ms = jnp.dot(x2 * x2, bd, preferred_element_type=F32, precision=HI)
    return x2 * lax.rsqrt(ms + EPS) * gain2


def _kv_prep_kernel(kv_ref, bd_ref, kn_ref, ko_ref, vt_ref):
    d = ko_ref.shape[1]
    for cb in range(d // LANES):
        cols = slice(cb * LANES, (cb + 1) * LANES)
        ko_ref[:, cols] = _pair_headnorm(kv_ref[:, cols], bd_ref[...], kn_ref[...]).astype(BF16)
        vt_ref[cols, :] = kv_ref[:, d + cb * LANES:d + (cb + 1) * LANES].T.astype(BF16)


def fox_kv_prep(kv, bd, k_gain2, tm=512):
    t, d2 = kv.shape
    d = d2 // 2
    return pl.pallas_call(
        _kv_prep_kernel,
        grid=(t // tm,),
        in_specs=[
            pl.BlockSpec((tm, d2), lambda i: (i, 0)),
            pl.BlockSpec((LANES, LANES), lambda i: (0, 0)),
            pl.BlockSpec((1, LANES), lambda i: (0, 0)),
        ],
        out_specs=[pl.BlockSpec((tm, d), lambda i: (i, 0)), pl.BlockSpec((d, tm), lambda i: (0, i))],
        out_shape=[jax.ShapeDtypeStruct((t, d), BF16), jax.ShapeDtypeStruct((d, t), BF16)],
        compiler_params=_params("parallel"),
        name="fox_kv_prep",
    )(kv, bd, k_gain2)


def _fox_attn_kernel(qi_ref, ki_ref, q_ref, k_ref, kb_ref, vt_ref, bd_ref, qn_ref, o_ref,
                     qs_ref, m_ref, l_ref, acc_ref):
    qi = qi_ref[pl.program_id(2)]
    ki = ki_ref[pl.program_id(2)]
    tq = q_ref.shape[0]
    tk = k_ref.shape[0]
    npb = q_ref.shape[1] // LANES

    @pl.when(ki == 0)
    def _():
        for pr in range(npb):
            cols = slice(pr * LANES, (pr + 1) * LANES)
            qt = (_pair_headnorm(q_ref[:, cols], bd_ref[...], qn_ref[...]) * (FOX_HD ** -0.5)).T
            row = lax.broadcasted_iota(jnp.int32, qt.shape, 0)
            for j in range(2):
                top = jnp.where((row >= j * FOX_HD) & (row < (j + 1) * FOX_HD), qt, 0.0)
                lo = j * FOX_BIAS_PIECES
                bias = jnp.where((row >= lo) & (row < lo + FOX_BIAS_PIECES), -1.0, 0.0)
                qs_ref[2 * pr + j] = jnp.concatenate([top, bias], axis=0).astype(BF16)
        m_ref[...] = jnp.full_like(m_ref, NEG_INF)
        l_ref[...] = jnp.zeros_like(l_ref)
        acc_ref[...] = jnp.zeros_like(acc_ref)

    def update(diagonal):
        if diagonal:
            causal = (lax.broadcasted_iota(jnp.int32, (tk, tq), 0) <= lax.broadcasted_iota(jnp.int32, (tk, tq), 1))
        for pr in range(npb):
            cols = slice(pr * LANES, (pr + 1) * LANES)
            kext = jnp.concatenate([k_ref[:, cols], kb_ref[:, cols]], axis=1)
            vt = vt_ref[cols, :]
            for j in range(2):
                h = 2 * pr + j
                st = jnp.dot(kext, qs_ref[h], preferred_element_type=F32)
                if diagonal:
                    st = jnp.where(causal, st, NEG_INF)
                m_prev = m_ref[h]
                m_new = jnp.maximum(m_prev, jnp.max(st, axis=0, keepdims=True))
                alpha = jnp.exp(m_prev - m_new)
                p = jnp.exp(st - m_new)
                l_ref[h] = alpha * l_ref[h] + jnp.sum(p, axis=0, keepdims=True)
                acc_ref[h] = alpha * acc_ref[h] + jnp.dot(vt, p.astype(BF16), preferred_element_type=F32)
                m_ref[h] = m_new

    @pl.when(ki < qi)
    def _():
        update(False)

    @pl.when(ki == qi)
    def _():
        update(True)
        row = lax.broadcasted_iota(jnp.int32, (LANES, tq), 0)
        for pr in range(npb):
            h0, h1 = 2 * pr, 2 * pr + 1
            o_ref[:, pr * LANES:(pr + 1) * LANES] = jnp.where(
                row < FOX_HD, acc_ref[h0] / l_ref[h0], acc_ref[h1] / l_ref[h1]).T


def fox_attention(qg, kn, kb, vt, bd, q_gain2, bsz, seq, tq=512, heads_per_step=4):
    t = qg.shape[0]
    d = FOX_HEADS * FOX_HD
    nq = seq // tq
    hs = heads_per_step
    wcol = hs * FOX_HD
    pairs = [(i, j) for i in range(nq) for j in range(i + 1)]
    qi_tab = jnp.asarray([p[0] for p in pairs], jnp.int32)
    ki_tab = jnp.asarray([p[1] for p in pairs], jnp.int32)
    qblk = lambda b, h, s, qt, kt: (b * nq + qt[s], h)
    kblk = lambda b, h, s, qt, kt: (b * nq + kt[s], h)
    const = lambda b, h, s, qt, kt: (0, 0)
    grid_spec = pltpu.PrefetchScalarGridSpec(
        num_scalar_prefetch=2,
        grid=(bsz, FOX_HEADS // hs, len(pairs)),
        in_specs=[
            pl.BlockSpec((tq, wcol), qblk),
            pl.BlockSpec((tq, wcol), kblk),
            pl.BlockSpec((tq, wcol), kblk),
            pl.BlockSpec((wcol, tq), lambda b, h, s, qt, kt: (h, b * nq + kt[s])),
            pl.BlockSpec((LANES, LANES), const),
            pl.BlockSpec((1, LANES), const),
        ],
        out_specs=pl.BlockSpec((tq, wcol), qblk),
        scratch_shapes=[
            pltpu.VMEM((hs, 2 * LANES, tq), BF16),
            pltpu.VMEM((hs, 1, tq), F32),
            pltpu.VMEM((hs, 1, tq), F32),
            pltpu.VMEM((hs, LANES, tq), F32),
        ],
    )
    return pl.pallas_call(
        _fox_attn_kernel,
        grid_spec=grid_spec,
        out_shape=jax.ShapeDtypeStruct((t, d), F32),
        compiler_params=_params("parallel", "parallel", "arbitrary"),
        name="fox_attention",
    )(qi_tab, ki_tab, qg, kn, kb, vt, bd, q_gain2)


SUBLANES = 8


def _merge_desc(v):
    n = len(v)
    if n == 1:
        return v
    half = n // 2
    hi = [jnp.maximum(v[i], v[i + half]) for i in range(half)]
    lo = [jnp.minimum(v[i], v[i + half]) for i in range(half)]
    return _merge_desc(hi) + _merge_desc(lo)


def _sort_desc(v):
    n = len(v)
    if n == 1:
        return v
    return _merge_desc(_sort_desc(v[:n // 2]) + _sort_desc(v[n // 2:])[::-1])


def _top16_values(groups):
    lists = _sort_desc(groups)
    shift = SUBLANES // 2
    while shift >= 1:
        partner = [pltpu.roll(a, shift, 0) for a in lists]
        if len(lists) < PEER_TOPK:
            lists = _merge_desc(lists + partner[::-1])
        else:
            n = len(lists)
            lists = _merge_desc([jnp.maximum(lists[i], partner[n - 1 - i]) for i in range(n)])
        shift //= 2
    return lists


_CAND_PAIRS = [(i, j) for i in range(PEER_TOPK) for j in range(PEER_TOPK) if (i + 1) * (j + 1) <= PEER_TOPK]


def _split_bf16(x):
    hi = x.astype(BF16)
    return hi, (x - hi.astype(F32)).astype(BF16)


def _peer_topk_kernel(x_ref, sh_ref, sc_ref, wqh_ref, wql_ref, sk_ref, ht_ref, p0s_ref, p1_ref, th_ref, qt_ref):
    h = _adaln_rows(x_ref[...], sh_ref[0], sc_ref[0])
    ht = h.T
    ht_ref[...] = ht.astype(BF16)
    dq = PEER_NKEYS
    ngroups = PEER_NKEYS // SUBLANES
    tb = ht.shape[1]
    sub = lax.broadcasted_iota(jnp.int32, (SUBLANES, tb), 0)
    thetas = []
    ht_hi, ht_lo = _split_bf16(ht)
    qt_ref[...] = (jnp.dot(wqh_ref[...], ht_hi, preferred_element_type=F32)
                   + jnp.dot(wqh_ref[...], ht_lo, preferred_element_type=F32)
                   + jnp.dot(wql_ref[...], ht_hi, preferred_element_type=F32))
    for head in range(PEER_HEADS):
        groups, tops = [], []
        for half in range(2):
            hp = 2 * head + half
            st = jnp.dot(sk_ref[hp], qt_ref[hp * dq:(hp + 1) * dq, :],
                         preferred_element_type=F32, precision=HI)
            g = [st[SUBLANES * k:SUBLANES * (k + 1), :] for k in range(ngroups)]
            m = functools.reduce(jnp.maximum, g)
            for shift in (4, 2, 1):
                m = jnp.maximum(m, pltpu.roll(m, shift, 0))
            g = [jnp.exp(v - m) for v in g]
            groups.append(g)
            tops.append(_top16_values(g))
        p0, p1 = tops

        def top_products(a, b):
            packed = []
            for k in range(0, len(_CAND_PAIRS), SUBLANES):
                acc = jnp.zeros((SUBLANES, tb), F32)
                for s_, (i, j) in enumerate(_CAND_PAIRS[k:k + SUBLANES]):
                    acc = jnp.where(sub == s_, a[i] * b[j], acc)
                packed.append(acc)
            while len(packed) & (len(packed) - 1):
                packed.append(jnp.zeros((SUBLANES, tb), F32))
            return _top16_values(packed)

        inv_z = 1.0 / functools.reduce(jnp.add, top_products(p0, p1))
        p0s = [v * inv_z for v in p0]
        thetas.append(top_products(p0s, p1)[PEER_TOPK - 1][0:1, :])
        for half, ref in enumerate((p0s_ref, p1_ref)):
            t16 = tops[half][PEER_TOPK - 1]
            for k in range(ngroups):
                g = groups[half][k]
                tab = jnp.where(g >= t16, g, 0.0)
                ref[head, SUBLANES * k:SUBLANES * (k + 1), :] = tab * inv_z if half == 0 else tab
    th_ref[...] = jnp.concatenate(thetas, axis=0)


def peer_topk(xf, shift, scale, wq_t, sub_keys, seq, tb=256):
    wq_hi, wq_lo = _split_bf16(wq_t)
    t, d = xf.shape
    per_seq = seq // tb
    nh, nk = PEER_HEADS, PEER_NKEYS
    tab = jax.ShapeDtypeStruct((nh, nk, t), F32)
    tab_spec = pl.BlockSpec((nh, nk, tb), lambda i: (0, 0, i))
    return pl.pallas_call(
        _peer_topk_kernel,
        grid=(t // tb,),
        in_specs=[
            pl.BlockSpec((tb, d), lambda i: (i, 0)),
            pl.BlockSpec((1, 1, d), lambda i: (i // per_seq, 0, 0)),
            pl.BlockSpec((1, 1, d), lambda i: (i // per_seq, 0, 0)),
            pl.BlockSpec(wq_t.shape, lambda i: (0, 0)),
            pl.BlockSpec(wq_t.shape, lambda i: (0, 0)),
            pl.BlockSpec(sub_keys.shape, lambda i: (0, 0, 0)),
        ],
        out_specs=[
            pl.BlockSpec((d, tb), lambda i: (0, i)),
            tab_spec, tab_spec,
            pl.BlockSpec((nh, tb), lambda i: (0, i)),
        ],
        out_shape=[jax.ShapeDtypeStruct((d, t), BF16), tab, tab, jax.ShapeDtypeStruct((nh, t), F32)],
        scratch_shapes=[pltpu.VMEM((wq_t.shape[0], tb), F32)],
        compiler_params=_params("parallel"),
        name="peer_topk",
    )(xf, shift, scale, wq_hi, wq_lo, sub_keys)


def _peer_dense_kernel(ht_ref, u_ref, vt_ref, p0s_ref, p1_ref, th_ref, x_ref, g2_ref, out_ref,
                       at_ref, wt_ref, acc_ref, *, tiles):
    e = pl.program_id(1)
    nk = PEER_NKEYS
    slot = e % 2
    prev = 1 - slot
    nsub, _, sub = wt_ref.shape

    def u_matmul(s):
        return jnp.dot(u_ref[...], ht_ref[:, s * sub:(s + 1) * sub], preferred_element_type=F32)

    @pl.when(e == 0)
    def _():
        acc_ref[...] = jnp.zeros_like(acc_ref)
        for s in range(nsub):
            at_ref[0, s] = u_matmul(s)

    @pl.when(e > 0)
    def _():
        for s in range(nsub):
            cols = slice(s * sub, (s + 1) * sub)
            a_new = u_matmul(s)
            for i in range(tiles):
                rows = slice(i * nk, (i + 1) * nk)
                a = at_ref[prev, s, rows, :]
                act = 0.5 * a * (1.0 + lax.erf(a * (1.0 / math.sqrt(2.0))))
                g = None
                for head in range(PEER_HEADS):
                    w = p0s_ref[head, i:i + 1, cols] * p1_ref[head, :, cols]
                    sel = w >= th_ref[head:head + 1, cols]
                    g = jnp.where(sel, w, 0.0) if g is None else jnp.where(sel, g + w, g)
                wt_ref[s, rows, :] = (g * act).astype(BF16)
            acc_ref[s] += jnp.dot(vt_ref[0], wt_ref[s], preferred_element_type=F32)
            at_ref[slot, s] = a_new

    @pl.when(e == pl.num_programs(1) - 1)
    def _():
        for s in range(nsub):
            rows = slice(s * sub, (s + 1) * sub)
            out_ref[rows, :] = x_ref[rows, :] + g2_ref[0] * acc_ref[s].T


def peer_dense(ht, u_bf16, vt_chunks, p0s, p1, theta, xf, g2, seq, tb=512, sub=256):
    t, d = xf.shape
    nchunk, _, ec = vt_chunks.shape
    per_seq = seq // tb
    nh, nk = PEER_HEADS, PEER_NKEYS
    tiles = ec // nk
    return pl.pallas_call(
        functools.partial(_peer_dense_kernel, tiles=tiles),
        grid=(t // tb, nchunk + 1),
        in_specs=[
            pl.BlockSpec((d, tb), lambda i, e: (0, i)),
            pl.BlockSpec((ec, d), lambda i, e: (jnp.minimum(e, nchunk - 1), 0)),
            pl.BlockSpec((1, d, ec), lambda i, e: (jnp.maximum(e - 1, 0), 0, 0)),
            pl.BlockSpec((nh, tiles, tb), lambda i, e: (0, jnp.maximum(e - 1, 0), i)),
            pl.BlockSpec((nh, nk, tb), lambda i, e: (0, 0, i)),
            pl.BlockSpec((nh, tb), lambda i, e: (0, i)),
            pl.BlockSpec((tb, d), lambda i, e: (i, 0)),
            pl.BlockSpec((1, 1, d), lambda i, e: (i // per_seq, 0, 0)),
        ],
        out_specs=pl.BlockSpec((tb, d), lambda i, e: (i, 0)),
        out_shape=jax.ShapeDtypeStruct((t, d), F32),
        scratch_shapes=[
            pltpu.VMEM((2, tb // sub, ec, sub), F32),
            pltpu.VMEM((tb // sub, ec, sub), BF16),
            pltpu.VMEM((tb // sub, d, sub), F32),
        ],
        compiler_params=pltpu.CompilerParams(dimension_semantics=("parallel", "arbitrary"),
                                             vmem_limit_bytes=PEER_DENSE_VMEM_BYTES),
        name="peer_dense",
    )(ht, u_bf16, vt_chunks, p0s, p1, theta, xf, g2)


def _pad_cols(w, n):
    return jnp.pad(w, ((0, 0), (0, n - w.shape[1])))


def kernel(x, c, mod_w, mod_b, gla_w_in, gla_w_gate2, gla_b_gate, gla_o_norm, gla_w_out, kv_mod_w, kv_mod_b, fox_w_kvf, fox_b_f, fox_k_norm, fox_w_qg, fox_q_norm, fox_w_out, peer_w_q, peer_sub_keys, peer_u, peer_v):
    bsz, seq, d = x.shape
    depth = mod_w.shape[0]
    n_gla = gla_w_in.shape[0]
    xf = x.reshape(bsz * seq, d)
    gla_dk = GLA_HEADS * GLA_HK
    gla_main = 2 * gla_dk + 2 * GLA_HEADS * GLA_HV

    shared = None
    for layer in range(depth):
        if layer == n_gla:
            kv_mod = mod_matmul(c, kv_mod_w, kv_mod_b).reshape(bsz, 2, 1, d)
            kv_sh, kv_sc = kv_mod[:, 0], kv_mod[:, 1]
            kv = adaln_matmul(xf, kv_sh, kv_sc, fox_w_kvf[:, :2 * d].astype(BF16), seq)
            pos = jnp.arange(d) % LANES
            src = 2 * (jnp.arange(d) // LANES) + pos // FOX_BIAS_PIECES
            used = pos < FOX_BIAS_LANES
            sel = ((src[None, :] == jnp.arange(LANES)[:, None]) & used[None, :]).astype(F32)
            f = adaln_matmul(xf, kv_sh, kv_sc, _pad_cols(fox_w_kvf[:, 2 * d:], LANES), seq)
            bf_pad = jnp.pad(fox_b_f, (0, LANES - FOX_HEADS)).reshape(1, LANES)
            kb = forget_cumsum(f, bf_pad, sel, bsz, seq)
            head_avg = jnp.kron(jnp.eye(LANES // FOX_HD, dtype=F32), jnp.full((FOX_HD, FOX_HD), 1.0 / FOX_HD, F32))
            kn, vt = fox_kv_prep(kv, head_avg, jnp.tile(fox_k_norm, LANES // FOX_HD).reshape(1, LANES))
            shared = (kn, kb, vt, head_avg)

        mod = mod_matmul(c, mod_w[layer], mod_b[layer]).reshape(bsz, 6, 1, d)
        sh1, sc1, g1, sh2, sc2, g2 = (mod[:, i] for i in range(6))

        if layer < n_gla:
            w_in = gla_w_in[layer]
            proj = adaln_matmul(xf, sh1, sc1, w_in[:, :gla_main].astype(BF16), seq)
            glow = adaln_matmul(xf, sh1, sc1, _pad_cols(w_in[:, gla_main:], LANES), seq)
            w2p = jnp.pad(gla_w_gate2[layer], ((0, LANES - GLA_GATE_RANK), (0, 0)))
            o = gla_recurrence(proj, glow, w2p, gla_b_gate[layer].reshape(1, gla_dk), bsz, seq)
            xf = mixer_out(o, proj, (2 * gla_dk + GLA_HEADS * GLA_HV) // d, xf, g1,
                           gla_w_out[layer].astype(BF16), seq, o_norm=gla_o_norm[layer])
        else:
            j = layer - n_gla
            kn, kb, vt, head_avg = shared
            qg = adaln_matmul(xf, sh1, sc1, fox_w_qg[j].astype(BF16), seq)
            o = fox_attention(qg, kn, kb, vt, head_avg,
                              jnp.tile(fox_q_norm[j], LANES // FOX_HD).reshape(1, LANES), bsz, seq)
            xf = mixer_out(o, qg, 1, xf, g1, fox_w_out[j].astype(BF16), seq)

        sub_keys = peer_sub_keys[layer].reshape(2 * PEER_HEADS, PEER_NKEYS, -1)
        ht, p0s, p1, theta = peer_topk(xf, sh2, sc2, peer_w_q[layer].T, sub_keys, seq)
        vt_chunks = peer_v[layer].astype(BF16).reshape(-1, PEER_CHUNK, d).transpose(0, 2, 1)
        xf = peer_dense(ht, peer_u[layer].astype(BF16), vt_chunks, p0s, p1, theta, xf, g2, seq)
    return xf.reshape(bsz, seq, d)
```

```python
import functools
import math

import jax
import jax.numpy as jnp
from jax import lax
from jax.experimental import pallas as pl
from jax.experimental.pallas import tpu as pltpu

F32 = jnp.float32
BF16 = jnp.bfloat16
HI = lax.Precision.HIGHEST
EPS = 1e-6
NEG_INF = float("-inf")

VMEM_LIMIT_BYTES = 48 * 1024 * 1024
LANES = 128

GLA_HEADS = 4
GLA_HK = 128
GLA_HV = 256
GLA_GATE_RANK = 16
GLA_GATE_NORM = 16.0
GLA_CHUNK = 64

FOX_HD = 64
FOX_HEADS = 16

PEER_HEADS = 8
PEER_NKEYS = 128
PEER_TOPK = 16
PEER_CHUNK = 2048
PEER_DENSE_VMEM_BYTES = 56 * 1024 * 1024


def _params(*sem):
    return pltpu.CompilerParams(dimension_semantics=sem, vmem_limit_bytes=VMEM_LIMIT_BYTES)


def _adaln_rows(x, shift, scale):
    ms = jnp.mean(x * x, axis=-1, keepdims=True)
    return x * lax.rsqrt(ms + EPS) * (1.0 + scale) + shift


def _mod_kernel(c_ref, w_ref, b_ref, o_ref):
    o_ref[...] = jnp.dot(c_ref[...], w_ref[...], preferred_element_type=F32, precision=HI) + b_ref[...]


def mod_matmul(c, w, b, tn=512):
    bsz, d = c.shape
    n = w.shape[1]
    return pl.pallas_call(
        _mod_kernel,
        grid=(n // tn,),
        in_specs=[
            pl.BlockSpec((bsz, d), lambda j: (0, 0)),
            pl.BlockSpec((d, tn), lambda j: (0, j)),
            pl.BlockSpec((1, tn), lambda j: (0, j)),
        ],
        out_specs=pl.BlockSpec((bsz, tn), lambda j: (0, j)),
        out_shape=jax.ShapeDtypeStruct((bsz, n), F32),
        compiler_params=_params("parallel"),
        name="mod_matmul",
    )(c, w, b.reshape(1, n))


def _adaln_mm_kernel(x_ref, sh_ref, sc_ref, w_ref, o_ref, h_ref):
    @pl.when(pl.program_id(1) == 0)
    def _():
        h_ref[...] = _adaln_rows(x_ref[...], sh_ref[0], sc_ref[0]).astype(h_ref.dtype)

    if w_ref.dtype == BF16:
        acc = jnp.dot(h_ref[...], w_ref[...], preferred_element_type=F32)
    else:
        acc = jnp.dot(h_ref[...], w_ref[...], preferred_element_type=F32, precision=HI)
    o_ref[...] = acc.astype(o_ref.dtype)


def adaln_matmul(xf, shift, scale, w, seq, tm=512, tn=512):
    t, d = xf.shape
    n = w.shape[1]
    tn = min(tn, n)
    per_seq = seq // tm
    return pl.pallas_call(
        _adaln_mm_kernel,
        grid=(t // tm, n // tn),
        in_specs=[
            pl.BlockSpec((tm, d), lambda i, j: (i, 0)),
            pl.BlockSpec((1, 1, d), lambda i, j: (i // per_seq, 0, 0)),
            pl.BlockSpec((1, 1, d), lambda i, j: (i // per_seq, 0, 0)),
            pl.BlockSpec((d, tn), lambda i, j: (0, j)),
        ],
        out_specs=pl.BlockSpec((tm, tn), lambda i, j: (i, j)),
        out_shape=jax.ShapeDtypeStruct((t, n), F32),
        scratch_shapes=[pltpu.VMEM((tm, d), w.dtype)],
        compiler_params=_params("parallel", "arbitrary"),
        name="adaln_matmul",
    )(xf, shift, scale, w)


GLA_SUB = 16


def _gla_kernel(q_ref, k_ref, v_ref, gl_ref, w2_ref, bg_ref, o_ref, st_ref, *, nchunk):
    c, sb = GLA_CHUNK, GLA_SUB
    nt = (((1,), (1,)), ((), ()))

    @pl.when(pl.program_id(2) == 0)
    def _():
        st_ref[...] = jnp.zeros_like(st_ref)

    row_c = lax.broadcasted_iota(jnp.int32, (c, GLA_HK), 0)
    row_b = lax.broadcasted_iota(jnp.int32, (sb, GLA_HK), 0)
    lane_b = lax.broadcasted_iota(jnp.int32, (sb, GLA_HK), 1)
    tri = (lax.broadcasted_iota(jnp.int32, (c, c), 1) <= lax.broadcasted_iota(jnp.int32, (c, c), 0)).astype(F32)

    def head_chunk(r0, hh):
        kc = slice(hh * GLA_HK, (hh + 1) * GLA_HK)
        vc = slice(hh * GLA_HV, (hh + 1) * GLA_HV)
        q = q_ref[pl.ds(r0, c), kc] * (GLA_HK ** -0.5)
        k = k_ref[pl.ds(r0, c), kc]
        v = v_ref[pl.ds(r0, c), vc].astype(BF16)
        pre = jnp.dot(gl_ref[pl.ds(r0, c), :], w2_ref[:, kc], preferred_element_type=F32, precision=HI) + bg_ref[:, kc]
        gk = jax.nn.log_sigmoid(pre) * (1.0 / GLA_GATE_NORM)
        b = jnp.dot(tri, gk, preferred_element_type=F32, precision=HI)
        st = st_ref[hh]
        o_inter = lax.dot_general((q * jnp.exp(b)).astype(BF16), st.astype(BF16), nt, preferred_element_type=F32)

        blocks = []
        for i in range(c // sb):
            lo = i * sb
            b_i, q_i, k_i = b[lo:lo + sb], q[lo:lo + sb], k[lo:lo + sb]
            d = jnp.zeros((sb, GLA_HK), F32)
            for j in range(sb):
                rel = jnp.where(row_b >= j, b_i - b_i[j:j + 1], NEG_INF)
                m = jnp.exp(rel) * (q_i * k_i[j:j + 1])
                d = jnp.where(lane_b == lo + j, jnp.sum(m, axis=-1, keepdims=True), d)
            att = d[:, :c]
            if i > 0:
                ref = b_i[0:1]
                qs = q_i * jnp.exp(b_i - ref)
                ks = k * jnp.exp(jnp.where(row_c < lo, ref - b, NEG_INF))
                att = att + lax.dot_general(qs.astype(BF16), ks.astype(BF16), nt, preferred_element_type=F32)
            blocks.append(att)
        attn = jnp.concatenate(blocks, axis=0)
        o_ref[pl.ds(r0, c), vc] = o_inter + jnp.dot(attn.astype(BF16), v, preferred_element_type=F32)
        b_last = b[c - 1:c, :]
        kdec = (k * jnp.exp(b_last - b)).astype(BF16)
        st_ref[hh] = st * jnp.exp(b_last) + lax.dot_general(
            v, kdec, (((0,), (0,)), ((), ())), preferred_element_type=F32)

    def chunk(ci, carry):
        r0 = pl.multiple_of(ci * c, c)
        for hh in range(st_ref.shape[0]):
            head_chunk(r0, hh)
        return carry

    lax.fori_loop(0, nchunk, chunk, 0)


def gla_recurrence(proj, glow, w2p, bg, bsz, seq, ts=512, heads_per_step=4):
    t = proj.shape[0]
    ns = seq // ts
    hs = heads_per_step
    wk, wv = hs * GLA_HK, hs * GLA_HV
    kcol = (GLA_HEADS * GLA_HK) // wk
    vcol = (2 * GLA_HEADS * GLA_HK) // wv
    return pl.pallas_call(
        functools.partial(_gla_kernel, nchunk=ts // GLA_CHUNK),
        grid=(bsz, GLA_HEADS // hs, ns),
        in_specs=[
            pl.BlockSpec((ts, wk), lambda b, h, s: (b * ns + s, h)),
            pl.BlockSpec((ts, wk), lambda b, h, s: (b * ns + s, kcol + h)),
            pl.BlockSpec((ts, wv), lambda b, h, s: (b * ns + s, vcol + h)),
            pl.BlockSpec((ts, LANES), lambda b, h, s: (b * ns + s, 0)),
            pl.BlockSpec((LANES, wk), lambda b, h, s: (0, h)),
            pl.BlockSpec((1, wk), lambda b, h, s: (0, h)),
        ],
        out_specs=pl.BlockSpec((ts, wv), lambda b, h, s: (b * ns + s, h)),
        out_shape=jax.ShapeDtypeStruct((t, GLA_HEADS * GLA_HV), F32),
        scratch_shapes=[pltpu.VMEM((hs, GLA_HV, GLA_HK), F32)],
        compiler_params=_params("parallel", "parallel", "arbitrary"),
        name="gla_recurrence",
    )(proj, proj, proj, glow, w2p, bg)


def _gla_out_kernel(o_ref, g_ref, x_ref, g1_ref, on_ref, w_ref, out_ref):
    parts = []
    for h in range(GLA_HEADS):
        oh = o_ref[:, h * GLA_HV:(h + 1) * GLA_HV]
        ms = jnp.mean(oh * oh, axis=-1, keepdims=True)
        parts.append(oh * lax.rsqrt(ms + EPS) * on_ref[...])
    y = jnp.concatenate(parts, axis=-1) * jax.nn.silu(g_ref[...])
    mix = jnp.dot(y.astype(BF16), w_ref[...], preferred_element_type=F32)
    out_ref[...] = x_ref[...] + g1_ref[0] * mix


def _fox_out_kernel(o_ref, g_ref, x_ref, g1_ref, w_ref, out_ref):
    y = o_ref[...] * jax.nn.sigmoid(g_ref[...])
    mix = jnp.dot(y.astype(BF16), w_ref[...], preferred_element_type=F32)
    out_ref[...] = x_ref[...] + g1_ref[0] * mix


def mixer_out(o, gsrc, gcol, xf, g1, w_bf16, seq, o_norm=None, tm=512):
    t, d = xf.shape
    per_seq = seq // tm
    row = lambda i: (i, 0)
    in_specs = [
        pl.BlockSpec((tm, d), row),
        pl.BlockSpec((tm, d), lambda i: (i, gcol)),
        pl.BlockSpec((tm, d), row),
        pl.BlockSpec((1, 1, d), lambda i: (i // per_seq, 0, 0)),
    ]
    args = [o, gsrc, xf, g1]
    if o_norm is not None:
        in_specs.append(pl.BlockSpec((1, GLA_HV), lambda i: (0, 0)))
        args.append(o_norm.reshape(1, GLA_HV))
        body = _gla_out_kernel
    else:
        body = _fox_out_kernel
    in_specs.append(pl.BlockSpec((d, d), lambda i: (0, 0)))
    args.append(w_bf16)
    return pl.pallas_call(
        body,
        grid=(t // tm,),
        in_specs=in_specs,
        out_specs=pl.BlockSpec((tm, d), row),
        out_shape=jax.ShapeDtypeStruct((t, d), F32),
        compiler_params=_params("parallel"),
        name="mixer_out",
    )(*args)


FOX_BIAS_PIECES = 3
FOX_BIAS_LANES = 2 * FOX_BIAS_PIECES


def _fgate_kernel(f_ref, bf_ref, sel_ref, kb_ref, carry_ref):
    tc = f_ref.shape[0]

    @pl.when(pl.program_id(1) == 0)
    def _():
        carry_ref[...] = jnp.zeros_like(carry_ref)

    logf = jax.nn.log_sigmoid(f_ref[...] + bf_ref[...])
    tri = (lax.broadcasted_iota(jnp.int32, (tc, tc), 1) <= lax.broadcasted_iota(jnp.int32, (tc, tc), 0)).astype(F32)
    cum_heads = jnp.dot(tri, logf, preferred_element_type=F32, precision=HI) + carry_ref[...]
    carry_ref[...] = cum_heads[tc - 1:tc, :]
    cum = jnp.dot(cum_heads, sel_ref[...], preferred_element_type=F32, precision=HI)
    hi = cum.astype(BF16).astype(F32)
    mid = (cum - hi).astype(BF16).astype(F32)
    lo = cum - hi - mid
    pos = lax.broadcasted_iota(jnp.int32, cum.shape, 1) % LANES
    piece = pos % FOX_BIAS_PIECES
    out = jnp.where(piece == 0, hi, jnp.where(piece == 1, mid, lo))
    kb_ref[...] = jnp.where(pos < FOX_BIAS_LANES, out, 0.0).astype(BF16)


def forget_cumsum(f, bf_pad, sel, bsz, seq, tc=256):
    t, nf = f.shape
    n = sel.shape[1]
    ns = seq // tc
    return pl.pallas_call(
        _fgate_kernel,
        grid=(bsz, ns),
        in_specs=[
            pl.BlockSpec((tc, nf), lambda b, s: (b * ns + s, 0)),
            pl.BlockSpec((1, nf), lambda b, s: (0, 0)),
            pl.BlockSpec((nf, n), lambda b, s: (0, 0)),
        ],
        out_specs=pl.BlockSpec((tc, n), lambda b, s: (b * ns + s, 0)),
        out_shape=jax.ShapeDtypeStruct((t, n), BF16),
        scratch_shapes=[pltpu.VMEM((1, nf), F32)],
        compiler_params=_params("parallel", "arbitrary"),
        name="forget_cumsum",
    )(f, bf_pad, sel)


def _pair_headnorm(x2, bd, gain2):
    ms = jnp.dot(x2 * x2, bd, preferred_element_type=F32, precision=HI)
    return x2 * lax.rsqrt(ms + EPS) * gain2


def _kv_prep_kernel(kv_ref, bd_ref, kn_ref, ko_ref, vt_ref):
    d = ko_ref.shape[1]
    for cb in range(d // LANES):
        cols = slice(cb * LANES, (cb + 1) * LANES)
        ko_ref[:, cols] = _pair_headnorm(kv_ref[:, cols], bd_ref[...], kn_ref[...]).astype(BF16)
        vt_ref[cols, :] = kv_ref[:, d + cb * LANES:d + (cb + 1) * LANES].T.astype(BF16)


def fox_kv_prep(kv, bd, k_gain2, tm=512):
    t, d2 = kv.shape
    d = d2 // 2
    return pl.pallas_call(
        _kv_prep_kernel,
        grid=(t // tm,),
        in_specs=[
            pl.BlockSpec((tm, d2), lambda i: (i, 0)),
            pl.BlockSpec((LANES, LANES), lambda i: (0, 0)),
            pl.BlockSpec((1, LANES), lambda i: (0, 0)),
        ],
        out_specs=[pl.BlockSpec((tm, d), lambda i: (i, 0)), pl.BlockSpec((d, tm), lambda i: (0, i))],
        out_shape=[jax.ShapeDtypeStruct((t, d), BF16), jax.ShapeDtypeStruct((d, t), BF16)],
        compiler_params=_params("parallel"),
        name="fox_kv_prep",
    )(kv, bd, k_gain2)


def _fox_attn_kernel(qi_ref, ki_ref, q_ref, k_ref, kb_ref, vt_ref, bd_ref, qn_ref, o_ref,
                     qs_ref, m_ref, l_ref, acc_ref):
    qi = qi_ref[pl.program_id(2)]
    ki = ki_ref[pl.program_id(2)]
    tq = q_ref.shape[0]
    tk = k_ref.shape[0]
    npb = q_ref.shape[1] // LANES

    @pl.when(ki == 0)
    def _():
        for pr in range(npb):
            cols = slice(pr * LANES, (pr + 1) * LANES)
            qt = (_pair_headnorm(q_ref[:, cols], bd_ref[...], qn_ref[...]) * (FOX_HD ** -0.5)).T
            row = lax.broadcasted_iota(jnp.int32, qt.shape, 0)
            for j in range(2):
                top = jnp.where((row >= j * FOX_HD) & (row < (j + 1) * FOX_HD), qt, 0.0)
                lo = j * FOX_BIAS_PIECES
                bias = jnp.where((row >= lo) & (row < lo + FOX_BIAS_PIECES), -1.0, 0.0)
                qs_ref[2 * pr + j] = jnp.concatenate([top, bias], axis=0).astype(BF16)
        m_ref[...] = jnp.full_like(m_ref, NEG_INF)
        l_ref[...] = jnp.zeros_like(l_ref)
        acc_ref[...] = jnp.zeros_like(acc_ref)

    def update(diagonal):
        if diagonal:
            causal = (lax.broadcasted_iota(jnp.int32, (tk, tq), 0) <= lax.broadcasted_iota(jnp.int32, (tk, tq), 1))
        for pr in range(npb):
            cols = slice(pr * LANES, (pr + 1) * LANES)
            kext = jnp.concatenate([k_ref[:, cols], kb_ref[:, cols]], axis=1)
            vt = vt_ref[cols, :]
            for j in range(2):
                h = 2 * pr + j
                st = jnp.dot(kext, qs_ref[h], preferred_element_type=F32)
                if diagonal:
                    st = jnp.where(causal, st, NEG_INF)
                m_prev = m_ref[h]
                m_new = jnp.maximum(m_prev, jnp.max(st, axis=0, keepdims=True))
                alpha = jnp.exp(m_prev - m_new)
                p = jnp.exp(st - m_new)
                l_ref[h] = alpha * l_ref[h] + jnp.sum(p, axis=0, keepdims=True)
                acc_ref[h] = alpha * acc_ref[h] + jnp.dot(vt, p.astype(BF16), preferred_element_type=F32)
                m_ref[h] = m_new

    @pl.when(ki < qi)
    def _():
        update(False)

    @pl.when(ki == qi)
    def _():
        update(True)
        row = lax.broadcasted_iota(jnp.int32, (LANES, tq), 0)
        for pr in range(npb):
            h0, h1 = 2 * pr, 2 * pr + 1
            o_ref[:, pr * LANES:(pr + 1) * LANES] = jnp.where(
                row < FOX_HD, acc_ref[h0] / l_ref[h0], acc_ref[h1] / l_ref[h1]).T


def fox_attention(qg, kn, kb, vt, bd, q_gain2, bsz, seq, tq=512, heads_per_step=8):
    t = qg.shape[0]
    d = FOX_HEADS * FOX_HD
    nq = seq // tq
    hs = heads_per_step
    wcol = hs * FOX_HD
    pairs = [(i, j) for i in range(nq) for j in range(i + 1)]
    qi_tab = jnp.asarray([p[0] for p in pairs], jnp.int32)
    ki_tab = jnp.asarray([p[1] for p in pairs], jnp.int32)
    qblk = lambda b, h, s, qt, kt: (b * nq + qt[s], h)
    kblk = lambda b, h, s, qt, kt: (b * nq + kt[s], h)
    const = lambda b, h, s, qt, kt: (0, 0)
    grid_spec = pltpu.PrefetchScalarGridSpec(
        num_scalar_prefetch=2,
        grid=(bsz, FOX_HEADS // hs, len(pairs)),
        in_specs=[
            pl.BlockSpec((tq, wcol), qblk),
            pl.BlockSpec((tq, wcol), kblk),
            pl.BlockSpec((tq, wcol), kblk),
            pl.BlockSpec((wcol, tq), lambda b, h, s, qt, kt: (h, b * nq + kt[s])),
            pl.BlockSpec((LANES, LANES), const),
            pl.BlockSpec((1, LANES), const),
        ],
        out_specs=pl.BlockSpec((tq, wcol), qblk),
        scratch_shapes=[
            pltpu.VMEM((hs, 2 * LANES, tq), BF16),
            pltpu.VMEM((hs, 1, tq), F32),
            pltpu.VMEM((hs, 1, tq), F32),
            pltpu.VMEM((hs, LANES, tq), F32),
        ],
    )
    return pl.pallas_call(
        _fox_attn_kernel,
        grid_spec=grid_spec,
        out_shape=jax.ShapeDtypeStruct((t, d), F32),
        compiler_params=_params("parallel", "parallel", "arbitrary"),
        name="fox_attention",
    )(qi_tab, ki_tab, qg, kn, kb, vt, bd, q_gain2)


SUBLANES = 8


def _merge_desc(v):
    n = len(v)
    if n == 1:
        return v
    half = n // 2
    hi = [jnp.maximum(v[i], v[i + half]) for i in range(half)]
    lo = [jnp.minimum(v[i], v[i + half]) for i in range(half)]
    return _merge_desc(hi) + _merge_desc(lo)


def _sort_desc(v):
    n = len(v)
    if n == 1:
        return v
    return _merge_desc(_sort_desc(v[:n // 2]) + _sort_desc(v[n // 2:])[::-1])


def _top16_values(groups):
    lists = _sort_desc(groups)
    shift = SUBLANES // 2
    while shift >= 1:
        partner = [pltpu.roll(a, shift, 0) for a in lists]
        if len(lists) < PEER_TOPK:
            lists = _merge_desc(lists + partner[::-1])
        else:
            n = len(lists)
            lists = _merge_desc([jnp.maximum(lists[i], partner[n - 1 - i]) for i in range(n)])
        shift //= 2
    return lists


_CAND_PAIRS = [(i, j) for i in range(PEER_TOPK) for j in range(PEER_TOPK) if (i + 1) * (j + 1) <= PEER_TOPK]


def _split_bf16(x):
    hi = x.astype(BF16)
    return hi, (x - hi.astype(F32)).astype(BF16)


def _peer_topk_kernel(x_ref, sh_ref, sc_ref, wqh_ref, wql_ref, sk_ref, ht_ref, p0s_ref, p1_ref, th_ref, qt_ref):
    h = _adaln_rows(x_ref[...], sh_ref[0], sc_ref[0])
    ht = h.T
    ht_ref[...] = ht.astype(BF16)
    dq = PEER_NKEYS
    ngroups = PEER_NKEYS // SUBLANES
    tb = ht.shape[1]
    sub = lax.broadcasted_iota(jnp.int32, (SUBLANES, tb), 0)
    thetas = []
    ht_hi, ht_lo = _split_bf16(ht)
    qt_ref[...] = (jnp.dot(wqh_ref[...], ht_hi, preferred_element_type=F32)
                   + jnp.dot(wqh_ref[...], ht_lo, preferred_element_type=F32)
                   + jnp.dot(wql_ref[...], ht_hi, preferred_element_type=F32))
    for head in range(PEER_HEADS):
        groups, tops = [], []
        for half in range(2):
            hp = 2 * head + half
            st = jnp.dot(sk_ref[hp], qt_ref[hp * dq:(hp + 1) * dq, :],
                         preferred_element_type=F32, precision=HI)
            g = [st[SUBLANES * k:SUBLANES * (k + 1), :] for k in range(ngroups)]
            m = functools.reduce(jnp.maximum, g)
            for shift in (4, 2, 1):
                m = jnp.maximum(m, pltpu.roll(m, shift, 0))
            g = [jnp.exp(v - m) for v in g]
            groups.append(g)
            tops.append(_top16_values(g))
        p0, p1 = tops

        def top_products(a, b):
            packed = []
            for k in range(0, len(_CAND_PAIRS), SUBLANES):
                acc = jnp.zeros((SUBLANES, tb), F32)
                for s_, (i, j) in enumerate(_CAND_PAIRS[k:k + SUBLANES]):
                    acc = jnp.where(sub == s_, a[i] * b[j], acc)
                packed.append(acc)
            while len(packed) & (len(packed) - 1):
                packed.append(jnp.zeros((SUBLANES, tb), F32))
            return _top16_values(packed)

        inv_z = 1.0 / functools.reduce(jnp.add, top_products(p0, p1))
        p0s = [v * inv_z for v in p0]
        thetas.append(top_products(p0s, p1)[PEER_TOPK - 1][0:1, :])
        for half, ref in enumerate((p0s_ref, p1_ref)):
            t16 = tops[half][PEER_TOPK - 1]
            for k in range(ngroups):
                g = groups[half][k]
                tab = jnp.where(g >= t16, g, 0.0)
                ref[head, SUBLANES * k:SUBLANES * (k + 1), :] = tab * inv_z if half == 0 else tab
    th_ref[...] = jnp.concatenate(thetas, axis=0)


def peer_topk(xf, shift, scale, wq_t, sub_keys, seq, tb=256):
    wq_hi, wq_lo = _split_bf16(wq_t)
    t, d = xf.shape
    per_seq = seq // tb
    nh, nk = PEER_HEADS, PEER_NKEYS
    tab = jax.ShapeDtypeStruct((nh, nk, t), F32)
    tab_spec = pl.BlockSpec((nh, nk, tb), lambda i: (0, 0, i))
    return pl.pallas_call(
        _peer_topk_kernel,
        grid=(t // tb,),
        in_specs=[
            pl.BlockSpec((tb, d), lambda i: (i, 0)),
            pl.BlockSpec((1, 1, d), lambda i: (i // per_seq, 0, 0)),
            pl.BlockSpec((1, 1, d), lambda i: (i // per_seq, 0, 0)),
            pl.BlockSpec(wq_t.shape, lambda i: (0, 0)),
            pl.BlockSpec(wq_t.shape, lambda i: (0, 0)),
            pl.BlockSpec(sub_keys.shape, lambda i: (0, 0, 0)),
        ],
        out_specs=[
            pl.BlockSpec((d, tb), lambda i: (0, i)),
            tab_spec, tab_spec,
            pl.BlockSpec((nh, tb), lambda i: (0, i)),
        ],
        out_shape=[jax.ShapeDtypeStruct((d, t), BF16), tab, tab, jax.ShapeDtypeStruct((nh, t), F32)],
        scratch_shapes=[pltpu.VMEM((wq_t.shape[0], tb), F32)],
        compiler_params=_params("parallel"),
        name="peer_topk",
    )(xf, shift, scale, wq_hi, wq_lo, sub_keys)


def _peer_dense_kernel(ht_ref, u_ref, vt_ref, p0s_ref, p1_ref, th_ref, x_ref, g2_ref, out_ref,
                       at_ref, wt_ref, acc_ref, *, tiles):
    e = pl.program_id(1)
    nk = PEER_NKEYS
    slot = e % 2
    prev = 1 - slot
    nsub, _, sub = wt_ref.shape

    def u_matmul(s):
        return jnp.dot(u_ref[...], ht_ref[:, s * sub:(s + 1) * sub], preferred_element_type=F32)

    @pl.when(e == 0)
    def _():
        acc_ref[...] = jnp.zeros_like(acc_ref)
        for s in range(nsub):
            at_ref[0, s] = u_matmul(s)

    @pl.when(e > 0)
    def _():
        for s in range(nsub):
            cols = slice(s * sub, (s + 1) * sub)
            a_new = u_matmul(s)
            for i in range(tiles):
                rows = slice(i * nk, (i + 1) * nk)
                a = at_ref[prev, s, rows, :]
                act = 0.5 * a * (1.0 + lax.erf(a * (1.0 / math.sqrt(2.0))))
                g = None
                for head in range(PEER_HEADS):
                    w = p0s_ref[head, i:i + 1, cols] * p1_ref[head, :, cols]
                    sel = w >= th_ref[head:head + 1, cols]
                    g = jnp.where(sel, w, 0.0) if g is None else jnp.where(sel, g + w, g)
                wt_ref[s, rows, :] = (g * act).astype(BF16)
            acc_ref[s] += jnp.dot(vt_ref[0], wt_ref[s], preferred_element_type=F32)
            at_ref[slot, s] = a_new

    @pl.when(e == pl.num_programs(1) - 1)
    def _():
        for s in range(nsub):
            rows = slice(s * sub, (s + 1) * sub)
            out_ref[rows, :] = x_ref[rows, :] + g2_ref[0] * acc_ref[s].T


def peer_dense(ht, u_bf16, vt_chunks, p0s, p1, theta, xf, g2, seq, tb=512, sub=256):
    t, d = xf.shape
    nchunk, _, ec = vt_chunks.shape
    per_seq = seq // tb
    nh, nk = PEER_HEADS, PEER_NKEYS
    tiles = ec // nk
    return pl.pallas_call(
        functools.partial(_peer_dense_kernel, tiles=tiles),
        grid=(t // tb, nchunk + 1),
        in_specs=[
            pl.BlockSpec((d, tb), lambda i, e: (0, i)),
            pl.BlockSpec((ec, d), lambda i, e: (jnp.minimum(e, nchunk - 1), 0)),
            pl.BlockSpec((1, d, ec), lambda i, e: (jnp.maximum(e - 1, 0), 0, 0)),
            pl.BlockSpec((nh, tiles, tb), lambda i, e: (0, jnp.maximum(e - 1, 0), i)),
            pl.BlockSpec((nh, nk, tb), lambda i, e: (0, 0, i)),
            pl.BlockSpec((nh, tb), lambda i, e: (0, i)),
            pl.BlockSpec((tb, d), lambda i, e: (i, 0)),
            pl.BlockSpec((1, 1, d), lambda i, e: (i // per_seq, 0, 0)),
        ],
        out_specs=pl.BlockSpec((tb, d), lambda i, e: (i, 0)),
        out_shape=jax.ShapeDtypeStruct((t, d), F32),
        scratch_shapes=[
            pltpu.VMEM((2, tb // sub, ec, sub), F32),
            pltpu.VMEM((tb // sub, ec, sub), BF16),
            pltpu.VMEM((tb // sub, d, sub), F32),
        ],
        compiler_params=pltpu.CompilerParams(dimension_semantics=("parallel", "arbitrary"),
                                             vmem_limit_bytes=PEER_DENSE_VMEM_BYTES),
        name="peer_dense",
    )(ht, u_bf16, vt_chunks, p0s, p1, theta, xf, g2)


def _pad_cols(w, n):
    return jnp.pad(w, ((0, 0), (0, n - w.shape[1])))


def kernel(x, c, mod_w, mod_b, gla_w_in, gla_w_gate2, gla_b_gate, gla_o_norm, gla_w_out, kv_mod_w, kv_mod_b, fox_w_kvf, fox_b_f, fox_k_norm, fox_w_qg, fox_q_norm, fox_w_out, peer_w_q, peer_sub_keys, peer_u, peer_v):
    bsz, seq, d = x.shape
    depth = mod_w.shape[0]
    n_gla = gla_w_in.shape[0]
    xf = x.reshape(bsz * seq, d)
    gla_dk = GLA_HEADS * GLA_HK
    gla_main = 2 * gla_dk + 2 * GLA_HEADS * GLA_HV

    shared = None
    for layer in range(depth):
        if layer == n_gla:
            kv_mod = mod_matmul(c, kv_mod_w, kv_mod_b).reshape(bsz, 2, 1, d)
            kv_sh, kv_sc = kv_mod[:, 0], kv_mod[:, 1]
            kv = adaln_matmul(xf, kv_sh, kv_sc, fox_w_kvf[:, :2 * d].astype(BF16), seq)
            pos = jnp.arange(d) % LANES
            src = 2 * (jnp.arange(d) // LANES) + pos // FOX_BIAS_PIECES
            used = pos < FOX_BIAS_LANES
            sel = ((src[None, :] == jnp.arange(LANES)[:, None]) & used[None, :]).astype(F32)
            f = adaln_matmul(xf, kv_sh, kv_sc, _pad_cols(fox_w_kvf[:, 2 * d:], LANES), seq)
            bf_pad = jnp.pad(fox_b_f, (0, LANES - FOX_HEADS)).reshape(1, LANES)
            kb = forget_cumsum(f, bf_pad, sel, bsz, seq)
            head_avg = jnp.kron(jnp.eye(LANES // FOX_HD, dtype=F32), jnp.full((FOX_HD, FOX_HD), 1.0 / FOX_HD, F32))
            kn, vt = fox_kv_prep(kv, head_avg, jnp.tile(fox_k_norm, LANES // FOX_HD).reshape(1, LANES))
            shared = (kn, kb, vt, head_avg)

        mod = mod_matmul(c, mod_w[layer], mod_b[layer]).reshape(bsz, 6, 1, d)
        sh1, sc1, g1, sh2, sc2, g2 = (mod[:, i] for i in range(6))

        if layer < n_gla:
            w_in = gla_w_in[layer]
            proj = adaln_matmul(xf, sh1, sc1, w_in[:, :gla_main].astype(BF16), seq)
            glow = adaln_matmul(xf, sh1, sc1, _pad_cols(w_in[:, gla_main:], LANES), seq)
            w2p = jnp.pad(gla_w_gate2[layer], ((0, LANES - GLA_GATE_RANK), (0, 0)))
            o = gla_recurrence(proj, glow, w2p, gla_b_gate[layer].reshape(1, gla_dk), bsz, seq)
            xf = mixer_out(o, proj, (2 * gla_dk + GLA_HEADS * GLA_HV) // d, xf, g1,
                           gla_w_out[layer].astype(BF16), seq, o_norm=gla_o_norm[layer])
        else:
            j = layer - n_gla
            kn, kb, vt, head_avg = shared
            qg = adaln_matmul(xf, sh1, sc1, fox_w_qg[j].astype(BF16), seq)
            o = fox_attention(qg, kn, kb, vt, head_avg,
                              jnp.tile(fox_q_norm[j], LANES // FOX_HD).reshape(1, LANES), bsz, seq)
            xf = mixer_out(o, qg, 1, xf, g1, fox_w_out[j].astype(BF16), seq)

        sub_keys = peer_sub_keys[layer].reshape(2 * PEER_HEADS, PEER_NKEYS, -1)
        ht, p0s, p1, theta = peer_topk(xf, sh2, sc2, peer_w_q[layer].T, sub_keys, seq)
        vt_chunks = peer_v[layer].astype(BF16).reshape(-1, PEER_CHUNK, d).transpose(0, 2, 1)
        xf = peer_dense(ht, peer_u[layer].astype(BF16), vt_chunks, p0s, p1, theta, xf, g2, seq)
    return xf.reshape(bsz, seq, d)
```

```python
import functools
import math

import jax
import jax.numpy as jnp
from jax import lax
from jax.experimental import pallas as pl
from jax.experimental.pallas import tpu as pltpu

F32 = jnp.float32
BF16 = jnp.bfloat16
HI = lax.Precision.HIGHEST
EPS = 1e-6
NEG_INF = float("-inf")

VMEM_LIMIT_BYTES = 48 * 1024 * 1024
LANES = 128

GLA_HEADS = 4
GLA_HK = 128
GLA_HV = 256
GLA_GATE_RANK = 16
GLA_GATE_NORM = 16.0
GLA_CHUNK = 64

FOX_HD = 64
FOX_HEADS = 16

PEER_HEADS = 8
PEER_NKEYS = 128
PEER_TOPK = 16
PEER_CHUNK = 2048
PEER_DENSE_VMEM_BYTES = 56 * 1024 * 1024


def _params(*sem):
    return pltpu.CompilerParams(dimension_semantics=sem, vmem_limit_bytes=VMEM_LIMIT_BYTES)


def _adaln_rows(x, shift, scale):
    ms = jnp.mean(x * x, axis=-1, keepdims=True)
    return x * lax.rsqrt(ms + EPS) * (1.0 + scale) + shift


def _mod_kernel(c_ref, w_ref, b_ref, o_ref):
    o_ref[...] = jnp.dot(c_ref[...], w_ref[...], preferred_element_type=F32, precision=HI) + b_ref[...]


def mod_matmul(c, w, b, tn=512):
    bsz, d = c.shape
    n = w.shape[1]
    return pl.pallas_call(
        _mod_kernel,
        grid=(n // tn,),
        in_specs=[
            pl.BlockSpec((bsz, d), lambda j: (0, 0)),
            pl.BlockSpec((d, tn), lambda j: (0, j)),
            pl.BlockSpec((1, tn), lambda j: (0, j)),
        ],
        out_specs=pl.BlockSpec((bsz, tn), lambda j: (0, j)),
        out_shape=jax.ShapeDtypeStruct((bsz, n), F32),
        compiler_params=_params("parallel"),
        name="mod_matmul",
    )(c, w, b.reshape(1, n))


def _adaln_mm_kernel(x_ref, sh_ref, sc_ref, w_ref, o_ref, h_ref):
    @pl.when(pl.program_id(1) == 0)
    def _():
        h_ref[...] = _adaln_rows(x_ref[...], sh_ref[0], sc_ref[0]).astype(h_ref.dtype)

    if w_ref.dtype == BF16:
        acc = jnp.dot(h_ref[...], w_ref[...], preferred_element_type=F32)
    else:
        acc = jnp.dot(h_ref[...], w_ref[...], preferred_element_type=F32, precision=HI)
    o_ref[...] = acc.astype(o_ref.dtype)


def adaln_matmul(xf, shift, scale, w, seq, tm=512, tn=1024):
    t, d = xf.shape
    n = w.shape[1]
    tn = min(tn, n)
    per_seq = seq // tm
    return pl.pallas_call(
        _adaln_mm_kernel,
        grid=(t // tm, n // tn),
        in_specs=[
            pl.BlockSpec((tm, d), lambda i, j: (i, 0)),
            pl.BlockSpec((1, 1, d), lambda i, j: (i // per_seq, 0, 0)),
            pl.BlockSpec((1, 1, d), lambda i, j: (i // per_seq, 0, 0)),
            pl.BlockSpec((d, tn), lambda i, j: (0, j)),
        ],
        out_specs=pl.BlockSpec((tm, tn), lambda i, j: (i, j)),
        out_shape=jax.ShapeDtypeStruct((t, n), F32),
        scratch_shapes=[pltpu.VMEM((tm, d), w.dtype)],
        compiler_params=_params("parallel", "arbitrary"),
        name="adaln_matmul",
    )(xf, shift, scale, w)


GLA_SUB = 16


def _gla_kernel(q_ref, k_ref, v_ref, gl_ref, w2_ref, bg_ref, o_ref, st_ref, *, nchunk):
    c, sb = GLA_CHUNK, GLA_SUB
    nt = (((1,), (1,)), ((), ()))

    @pl.when(pl.program_id(2) == 0)
    def _():
        st_ref[...] = jnp.zeros_like(st_ref)

    row_c = lax.broadcasted_iota(jnp.int32, (c, GLA_HK), 0)
    row_b = lax.broadcasted_iota(jnp.int32, (sb, GLA_HK), 0)
    lane_b = lax.broadcasted_iota(jnp.int32, (sb, GLA_HK), 1)
    tri = (lax.broadcasted_iota(jnp.int32, (c, c), 1) <= lax.broadcasted_iota(jnp.int32, (c, c), 0)).astype(F32)

    def head_chunk(r0, hh):
        kc = slice(hh * GLA_HK, (hh + 1) * GLA_HK)
        vc = slice(hh * GLA_HV, (hh + 1) * GLA_HV)
        q = q_ref[pl.ds(r0, c), kc] * (GLA_HK ** -0.5)
        k = k_ref[pl.ds(r0, c), kc]
        v = v_ref[pl.ds(r0, c), vc].astype(BF16)
        pre = jnp.dot(gl_ref[pl.ds(r0, c), :], w2_ref[:, kc], preferred_element_type=F32, precision=HI) + bg_ref[:, kc]
        gk = jax.nn.log_sigmoid(pre) * (1.0 / GLA_GATE_NORM)
        b = jnp.dot(tri, gk, preferred_element_type=F32, precision=HI)
        st = st_ref[hh]
        o_inter = lax.dot_general((q * jnp.exp(b)).astype(BF16), st.astype(BF16), nt, preferred_element_type=F32)

        blocks = []
        for i in range(c // sb):
            lo = i * sb
            b_i, q_i, k_i = b[lo:lo + sb], q[lo:lo + sb], k[lo:lo + sb]
            d = jnp.zeros((sb, GLA_HK), F32)
            for j in range(sb):
                rel = jnp.where(row_b >= j, b_i - b_i[j:j + 1], NEG_INF)
                m = jnp.exp(rel) * (q_i * k_i[j:j + 1])
                d = jnp.where(lane_b == lo + j, jnp.sum(m, axis=-1, keepdims=True), d)
            att = d[:, :c]
            if i > 0:
                ref = b_i[0:1]
                qs = q_i * jnp.exp(b_i - ref)
                ks = k * jnp.exp(jnp.where(row_c < lo, ref - b, NEG_INF))
                att = att + lax.dot_general(qs.astype(BF16), ks.astype(BF16), nt, preferred_element_type=F32)
            blocks.append(att)
        attn = jnp.concatenate(blocks, axis=0)
        o_ref[pl.ds(r0, c), vc] = o_inter + jnp.dot(attn.astype(BF16), v, preferred_element_type=F32)
        b_last = b[c - 1:c, :]
        kdec = (k * jnp.exp(b_last - b)).astype(BF16)
        st_ref[hh] = st * jnp.exp(b_last) + lax.dot_general(
            v, kdec, (((0,), (0,)), ((), ())), preferred_element_type=F32)

    def chunk(ci, carry):
        r0 = pl.multiple_of(ci * c, c)
        for hh in range(st_ref.shape[0]):
            head_chunk(r0, hh)
        return carry

    lax.fori_loop(0, nchunk, chunk, 0)


def gla_recurrence(proj, glow, w2p, bg, bsz, seq, ts=512, heads_per_step=4):
    t = proj.shape[0]
    ns = seq // ts
    hs = heads_per_step
    wk, wv = hs * GLA_HK, hs * GLA_HV
    kcol = (GLA_HEADS * GLA_HK) // wk
    vcol = (2 * GLA_HEADS * GLA_HK) // wv
    return pl.pallas_call(
        functools.partial(_gla_kernel, nchunk=ts // GLA_CHUNK),
        grid=(bsz, GLA_HEADS // hs, ns),
        in_specs=[
            pl.BlockSpec((ts, wk), lambda b, h, s: (b * ns + s, h)),
            pl.BlockSpec((ts, wk), lambda b, h, s: (b * ns + s, kcol + h)),
            pl.BlockSpec((ts, wv), lambda b, h, s: (b * ns + s, vcol + h)),
            pl.BlockSpec((ts, LANES), lambda b, h, s: (b * ns + s, 0)),
            pl.BlockSpec((LANES, wk), lambda b, h, s: (0, h)),
            pl.BlockSpec((1, wk), lambda b, h, s: (0, h)),
        ],
        out_specs=pl.BlockSpec((ts, wv), lambda b, h, s: (b * ns + s, h)),
        out_shape=jax.ShapeDtypeStruct((t, GLA_HEADS * GLA_HV), F32),
        scratch_shapes=[pltpu.VMEM((hs, GLA_HV, GLA_HK), F32)],
        compiler_params=_params("parallel", "parallel", "arbitrary"),
        name="gla_recurrence",
    )(proj, proj, proj, glow, w2p, bg)


def _gla_out_kernel(o_ref, g_ref, x_ref, g1_ref, on_ref, w_ref, out_ref):
    parts = []
    for h in range(GLA_HEADS):
        oh = o_ref[:, h * GLA_HV:(h + 1) * GLA_HV]
        ms = jnp.mean(oh * oh, axis=-1, keepdims=True)
        parts.append(oh * lax.rsqrt(ms + EPS) * on_ref[...])
    y = jnp.concatenate(parts, axis=-1) * jax.nn.silu(g_ref[...])
    mix = jnp.dot(y.astype(BF16), w_ref[...], preferred_element_type=F32)
    out_ref[...] = x_ref[...] + g1_ref[0] * mix


def _fox_out_kernel(o_ref, g_ref, x_ref, g1_ref, w_ref, out_ref):
    y = o_ref[...] * jax.nn.sigmoid(g_ref[...])
    mix = jnp.dot(y.astype(BF16), w_ref[...], preferred_element_type=F32)
    out_ref[...] = x_ref[...] + g1_ref[0] * mix


def mixer_out(o, gsrc, gcol, xf, g1, w_bf16, seq, o_norm=None, tm=512):
    t, d = xf.shape
    per_seq = seq // tm
    row = lambda i: (i, 0)
    in_specs = [
        pl.BlockSpec((tm, d), row),
        pl.BlockSpec((tm, d), lambda i: (i, gcol)),
        pl.BlockSpec((tm, d), row),
        pl.BlockSpec((1, 1, d), lambda i: (i // per_seq, 0, 0)),
    ]
    args = [o, gsrc, xf, g1]
    if o_norm is not None:
        in_specs.append(pl.BlockSpec((1, GLA_HV), lambda i: (0, 0)))
        args.append(o_norm.reshape(1, GLA_HV))
        body = _gla_out_kernel
    else:
        body = _fox_out_kernel
    in_specs.append(pl.BlockSpec((d, d), lambda i: (0, 0)))
    args.append(w_bf16)
    return pl.pallas_call(
        body,
        grid=(t // tm,),
        in_specs=in_specs,
        out_specs=pl.BlockSpec((tm, d), row),
        out_shape=jax.ShapeDtypeStruct((t, d), F32),
        compiler_params=_params("parallel"),
        name="mixer_out",
    )(*args)


FOX_BIAS_PIECES = 3
FOX_BIAS_LANES = 2 * FOX_BIAS_PIECES


def _fgate_kernel(f_ref, bf_ref, sel_ref, kb_ref, carry_ref):
    tc = f_ref.shape[0]

    @pl.when(pl.program_id(1) == 0)
    def _():
        carry_ref[...] = jnp.zeros_like(carry_ref)

    logf = jax.nn.log_sigmoid(f_ref[...] + bf_ref[...])
    tri = (lax.broadcasted_iota(jnp.int32, (tc, tc), 1) <= lax.broadcasted_iota(jnp.int32, (tc, tc), 0)).astype(F32)
    cum_heads = jnp.dot(tri, logf, preferred_element_type=F32, precision=HI) + carry_ref[...]
    carry_ref[...] = cum_heads[tc - 1:tc, :]
    cum = jnp.dot(cum_heads, sel_ref[...], preferred_element_type=F32, precision=HI)
    hi = cum.astype(BF16).astype(F32)
    mid = (cum - hi).astype(BF16).astype(F32)
    lo = cum - hi - mid
    pos = lax.broadcasted_iota(jnp.int32, cum.shape, 1) % LANES
    piece = pos % FOX_BIAS_PIECES
    out = jnp.where(piece == 0, hi, jnp.where(piece == 1, mid, lo))
    kb_ref[...] = jnp.where(pos < FOX_BIAS_LANES, out, 0.0).astype(BF16)


def forget_cumsum(f, bf_pad, sel, bsz, seq, tc=256):
    t, nf = f.shape
    n = sel.shape[1]
    ns = seq // tc
    return pl.pallas_call(
        _fgate_kernel,
        grid=(bsz, ns),
        in_specs=[
            pl.BlockSpec((tc, nf), lambda b, s: (b * ns + s, 0)),
            pl.BlockSpec((1, nf), lambda b, s: (0, 0)),
            pl.BlockSpec((nf, n), lambda b, s: (0, 0)),
        ],
        out_specs=pl.BlockSpec((tc, n), lambda b, s: (b * ns + s, 0)),
        out_shape=jax.ShapeDtypeStruct((t, n), BF16),
        scratch_shapes=[pltpu.VMEM((1, nf), F32)],
        compiler_params=_params("parallel", "arbitrary"),
        name="forget_cumsum",
    )(f, bf_pad, sel)


def _pair_headnorm(x2, bd, gain2):
    ms = jnp.dot(x2 * x2, bd, preferred_element_type=F32, precision=HI)
    return x2 * lax.rsqrt(ms + EPS) * gain2


def _kv_prep_kernel(kv_ref, bd_ref, kn_ref, ko_ref, vt_ref):
    d = ko_ref.shape[1]
    for cb in range(d // LANES):
        cols = slice(cb * LANES, (cb + 1) * LANES)
        ko_ref[:, cols] = _pair_headnorm(kv_ref[:, cols], bd_ref[...], kn_ref[...]).astype(BF16)
        vt_ref[cols, :] = kv_ref[:, d + cb * LANES:d + (cb + 1) * LANES].T.astype(BF16)


def fox_kv_prep(kv, bd, k_gain2, tm=512):
    t, d2 = kv.shape
    d = d2 // 2
    return pl.pallas_call(
        _kv_prep_kernel,
        grid=(t // tm,),
        in_specs=[
            pl.BlockSpec((tm, d2), lambda i: (i, 0)),
            pl.BlockSpec((LANES, LANES), lambda i: (0, 0)),
            pl.BlockSpec((1, LANES), lambda i: (0, 0)),
        ],
        out_specs=[pl.BlockSpec((tm, d), lambda i: (i, 0)), pl.BlockSpec((d, tm), lambda i: (0, i))],
        out_shape=[jax.ShapeDtypeStruct((t, d), BF16), jax.ShapeDtypeStruct((d, t), BF16)],
        compiler_params=_params("parallel"),
        name="fox_kv_prep",
    )(kv, bd, k_gain2)


def _fox_attn_kernel(qi_ref, ki_ref, q_ref, k_ref, kb_ref, vt_ref, bd_ref, qn_ref, o_ref,
                     qs_ref, m_ref, l_ref, acc_ref):
    qi = qi_ref[pl.program_id(2)]
    ki = ki_ref[pl.program_id(2)]
    tq = q_ref.shape[0]
    tk = k_ref.shape[0]
    npb = q_ref.shape[1] // LANES

    @pl.when(ki == 0)
    def _():
        for pr in range(npb):
            cols = slice(pr * LANES, (pr + 1) * LANES)
            qt = (_pair_headnorm(q_ref[:, cols], bd_ref[...], qn_ref[...]) * (FOX_HD ** -0.5)).T
            row = lax.broadcasted_iota(jnp.int32, qt.shape, 0)
            for j in range(2):
                top = jnp.where((row >= j * FOX_HD) & (row < (j + 1) * FOX_HD), qt, 0.0)
                lo = j * FOX_BIAS_PIECES
                bias = jnp.where((row >= lo) & (row < lo + FOX_BIAS_PIECES), -1.0, 0.0)
                qs_ref[2 * pr + j] = jnp.concatenate([top, bias], axis=0).astype(BF16)
        m_ref[...] = jnp.full_like(m_ref, NEG_INF)
        l_ref[...] = jnp.zeros_like(l_ref)
        acc_ref[...] = jnp.zeros_like(acc_ref)

    def update(diagonal):
        if diagonal:
            causal = (lax.broadcasted_iota(jnp.int32, (tk, tq), 0) <= lax.broadcasted_iota(jnp.int32, (tk, tq), 1))
        for pr in range(npb):
            cols = slice(pr * LANES, (pr + 1) * LANES)
            kext = jnp.concatenate([k_ref[:, cols], kb_ref[:, cols]], axis=1)
            vt = vt_ref[cols, :]
            for j in range(2):
                h = 2 * pr + j
                st = jnp.dot(kext, qs_ref[h], preferred_element_type=F32)
                if diagonal:
                    st = jnp.where(causal, st, NEG_INF)
                m_prev = m_ref[h]
                m_new = jnp.maximum(m_prev, jnp.max(st, axis=0, keepdims=True))
                alpha = jnp.exp(m_prev - m_new)
                p = jnp.exp(st - m_new)
                l_ref[h] = alpha * l_ref[h] + jnp.sum(p, axis=0, keepdims=True)
                acc_ref[h] = alpha * acc_ref[h] + jnp.dot(vt, p.astype(BF16), preferred_element_type=F32)
                m_ref[h] = m_new

    @pl.when(ki < qi)
    def _():
        update(False)

    @pl.when(ki == qi)
    def _():
        update(True)
        row = lax.broadcasted_iota(jnp.int32, (LANES, tq), 0)
        for pr in range(npb):
            h0, h1 = 2 * pr, 2 * pr + 1
            o_ref[:, pr * LANES:(pr + 1) * LANES] = jnp.where(
                row < FOX_HD, acc_ref[h0] / l_ref[h0], acc_ref[h1] / l_ref[h1]).T


def fox_attention(qg, kn, kb, vt, bd, q_gain2, bsz, seq, tq=512, heads_per_step=8):
    t = qg.shape[0]
    d = FOX_HEADS * FOX_HD
    nq = seq // tq
    hs = heads_per_step
    wcol = hs * FOX_HD
    pairs = [(i, j) for i in range(nq) for j in range(i + 1)]
    qi_tab = jnp.asarray([p[0] for p in pairs], jnp.int32)
    ki_tab = jnp.asarray([p[1] for p in pairs], jnp.int32)
    qblk = lambda b, h, s, qt, kt: (b * nq + qt[s], h)
    kblk = lambda b, h, s, qt, kt: (b * nq + kt[s], h)
    const = lambda b, h, s, qt, kt: (0, 0)
    grid_spec = pltpu.PrefetchScalarGridSpec(
        num_scalar_prefetch=2,
        grid=(bsz, FOX_HEADS // hs, len(pairs)),
        in_specs=[
            pl.BlockSpec((tq, wcol), qblk),
            pl.BlockSpec((tq, wcol), kblk),
            pl.BlockSpec((tq, wcol), kblk),
            pl.BlockSpec((wcol, tq), lambda b, h, s, qt, kt: (h, b * nq + kt[s])),
            pl.BlockSpec((LANES, LANES), const),
            pl.BlockSpec((1, LANES), const),
        ],
        out_specs=pl.BlockSpec((tq, wcol), qblk),
        scratch_shapes=[
            pltpu.VMEM((hs, 2 * LANES, tq), BF16),
            pltpu.VMEM((hs, 1, tq), F32),
            pltpu.VMEM((hs, 1, tq), F32),
            pltpu.VMEM((hs, LANES, tq), F32),
        ],
    )
    return pl.pallas_call(
        _fox_attn_kernel,
        grid_spec=grid_spec,
        out_shape=jax.ShapeDtypeStruct((t, d), F32),
        compiler_params=_params("parallel", "parallel", "arbitrary"),
        name="fox_attention",
    )(qi_tab, ki_tab, qg, kn, kb, vt, bd, q_gain2)


SUBLANES = 8


def _merge_desc(v):
    n = len(v)
    if n == 1:
        return v
    half = n // 2
    hi = [jnp.maximum(v[i], v[i + half]) for i in range(half)]
    lo = [jnp.minimum(v[i], v[i + half]) for i in range(half)]
    return _merge_desc(hi) + _merge_desc(lo)


def _sort_desc(v):
    n = len(v)
    if n == 1:
        return v
    return _merge_desc(_sort_desc(v[:n // 2]) + _sort_desc(v[n // 2:])[::-1])


def _top16_values(groups):
    lists = _sort_desc(groups)
    shift = SUBLANES // 2
    while shift >= 1:
        partner = [pltpu.roll(a, shift, 0) for a in lists]
        if len(lists) < PEER_TOPK:
            lists = _merge_desc(lists + partner[::-1])
        else:
            n = len(lists)
            lists = _merge_desc([jnp.maximum(lists[i], partner[n - 1 - i]) for i in range(n)])
        shift //= 2
    return lists


_CAND_PAIRS = [(i, j) for i in range(PEER_TOPK) for j in range(PEER_TOPK) if (i + 1) * (j + 1) <= PEER_TOPK]


def _split_bf16(x):
    hi = x.astype(BF16)
    return hi, (x - hi.astype(F32)).astype(BF16)


def _peer_topk_kernel(x_ref, sh_ref, sc_ref, wqh_ref, wql_ref, sk_ref, ht_ref, p0s_ref, p1_ref, th_ref, qt_ref):
    h = _adaln_rows(x_ref[...], sh_ref[0], sc_ref[0])
    ht = h.T
    ht_ref[...] = ht.astype(BF16)
    dq = PEER_NKEYS
    ngroups = PEER_NKEYS // SUBLANES
    tb = ht.shape[1]
    sub = lax.broadcasted_iota(jnp.int32, (SUBLANES, tb), 0)
    thetas = []
    ht_hi, ht_lo = _split_bf16(ht)
    qt_ref[...] = (jnp.dot(wqh_ref[...], ht_hi, preferred_element_type=F32)
                   + jnp.dot(wqh_ref[...], ht_lo, preferred_element_type=F32)
                   + jnp.dot(wql_ref[...], ht_hi, preferred_element_type=F32))
    for head in range(PEER_HEADS):
        groups, tops = [], []
        for half in range(2):
            hp = 2 * head + half
            st = jnp.dot(sk_ref[hp], qt_ref[hp * dq:(hp + 1) * dq, :],
                         preferred_element_type=F32, precision=HI)
            g = [st[SUBLANES * k:SUBLANES * (k + 1), :] for k in range(ngroups)]
            m = functools.reduce(jnp.maximum, g)
            for shift in (4, 2, 1):
                m = jnp.maximum(m, pltpu.roll(m, shift, 0))
            g = [jnp.exp(v - m) for v in g]
            groups.append(g)
            tops.append(_top16_values(g))
        p0, p1 = tops

        def top_products(a, b):
            packed = []
            for k in range(0, len(_CAND_PAIRS), SUBLANES):
                acc = jnp.zeros((SUBLANES, tb), F32)
                for s_, (i, j) in enumerate(_CAND_PAIRS[k:k + SUBLANES]):
                    acc = jnp.where(sub == s_, a[i] * b[j], acc)
                packed.append(acc)
            while len(packed) & (len(packed) - 1):
                packed.append(jnp.zeros((SUBLANES, tb), F32))
            return _top16_values(packed)

        inv_z = 1.0 / functools.reduce(jnp.add, top_products(p0, p1))
        p0s = [v * inv_z for v in p0]
        thetas.append(top_products(p0s, p1)[PEER_TOPK - 1][0:1, :])
        for half, ref in enumerate((p0s_ref, p1_ref)):
            t16 = tops[half][PEER_TOPK - 1]
            for k in range(ngroups):
                g = groups[half][k]
                tab = jnp.where(g >= t16, g, 0.0)
                ref[head, SUBLANES * k:SUBLANES * (k + 1), :] = tab * inv_z if half == 0 else tab
    th_ref[...] = jnp.concatenate(thetas, axis=0)


def peer_topk(xf, shift, scale, wq_t, sub_keys, seq, tb=256):
    wq_hi, wq_lo = _split_bf16(wq_t)
    t, d = xf.shape
    per_seq = seq // tb
    nh, nk = PEER_HEADS, PEER_NKEYS
    tab = jax.ShapeDtypeStruct((nh, nk, t), F32)
    tab_spec = pl.BlockSpec((nh, nk, tb), lambda i: (0, 0, i))
    return pl.pallas_call(
        _peer_topk_kernel,
        grid=(t // tb,),
        in_specs=[
            pl.BlockSpec((tb, d), lambda i: (i, 0)),
            pl.BlockSpec((1, 1, d), lambda i: (i // per_seq, 0, 0)),
            pl.BlockSpec((1, 1, d), lambda i: (i // per_seq, 0, 0)),
            pl.BlockSpec(wq_t.shape, lambda i: (0, 0)),
            pl.BlockSpec(wq_t.shape, lambda i: (0, 0)),
            pl.BlockSpec(sub_keys.shape, lambda i: (0, 0, 0)),
        ],
        out_specs=[
            pl.BlockSpec((d, tb), lambda i: (0, i)),
            tab_spec, tab_spec,
            pl.BlockSpec((nh, tb), lambda i: (0, i)),
        ],
        out_shape=[jax.ShapeDtypeStruct((d, t), BF16), tab, tab, jax.ShapeDtypeStruct((nh, t), F32)],
        scratch_shapes=[pltpu.VMEM((wq_t.shape[0], tb), F32)],
        compiler_params=_params("parallel"),
        name="peer_topk",
    )(xf, shift, scale, wq_hi, wq_lo, sub_keys)


def _peer_dense_kernel(ht_ref, u_ref, vt_ref, p0s_ref, p1_ref, th_ref, x_ref, g2_ref, out_ref,
                       at_ref, wt_ref, acc_ref, *, tiles):
    e = pl.program_id(1)
    nk = PEER_NKEYS
    slot = e % 2
    prev = 1 - slot
    nsub, _, sub = wt_ref.shape

    def u_matmul(s):
        return jnp.dot(u_ref[...], ht_ref[:, s * sub:(s + 1) * sub], preferred_element_type=F32)

    @pl.when(e == 0)
    def _():
        acc_ref[...] = jnp.zeros_like(acc_ref)
        for s in range(nsub):
            at_ref[0, s] = u_matmul(s)

    @pl.when(e > 0)
    def _():
        for s in range(nsub):
            cols = slice(s * sub, (s + 1) * sub)
            a_new = u_matmul(s)
            for i in range(tiles):
                rows = slice(i * nk, (i + 1) * nk)
                a = at_ref[prev, s, rows, :]
                act = 0.5 * a * (1.0 + lax.erf(a * (1.0 / math.sqrt(2.0))))
                g = None
                for head in range(PEER_HEADS):
                    w = p0s_ref[head, i:i + 1, cols] * p1_ref[head, :, cols]
                    sel = w >= th_ref[head:head + 1, cols]
                    g = jnp.where(sel, w, 0.0) if g is None else jnp.where(sel, g + w, g)
                wt_ref[s, rows, :] = (g * act).astype(BF16)
            acc_ref[s] += jnp.dot(vt_ref[0], wt_ref[s], preferred_element_type=F32)
            at_ref[slot, s] = a_new

    @pl.when(e == pl.num_programs(1) - 1)
    def _():
        for s in range(nsub):
            rows = slice(s * sub, (s + 1) * sub)
            out_ref[rows, :] = x_ref[rows, :] + g2_ref[0] * acc_ref[s].T


def peer_dense(ht, u_bf16, vt_chunks, p0s, p1, theta, xf, g2, seq, tb=512, sub=256):
    t, d = xf.shape
    nchunk, _, ec = vt_chunks.shape
    per_seq = seq // tb
    nh, nk = PEER_HEADS, PEER_NKEYS
    tiles = ec // nk
    return pl.pallas_call(
        functools.partial(_peer_dense_kernel, tiles=tiles),
        grid=(t // tb, nchunk + 1),
        in_specs=[
            pl.BlockSpec((d, tb), lambda i, e: (0, i)),
            pl.BlockSpec((ec, d), lambda i, e: (jnp.minimum(e, nchunk - 1), 0)),
            pl.BlockSpec((1, d, ec), lambda i, e: (jnp.maximum(e - 1, 0), 0, 0)),
            pl.BlockSpec((nh, tiles, tb), lambda i, e: (0, jnp.maximum(e - 1, 0), i)),
            pl.BlockSpec((nh, nk, tb), lambda i, e: (0, 0, i)),
            pl.BlockSpec((nh, tb), lambda i, e: (0, i)),
            pl.BlockSpec((tb, d), lambda i, e: (i, 0)),
            pl.BlockSpec((1, 1, d), lambda i, e: (i // per_seq, 0, 0)),
        ],
        out_specs=pl.BlockSpec((tb, d), lambda i, e: (i, 0)),
        out_shape=jax.ShapeDtypeStruct((t, d), F32),
        scratch_shapes=[
            pltpu.VMEM((2, tb // sub, ec, sub), F32),
            pltpu.VMEM((tb // sub, ec, sub), BF16),
            pltpu.VMEM((tb // sub, d, sub), F32),
        ],
        compiler_params=pltpu.CompilerParams(dimension_semantics=("parallel", "arbitrary"),
                                             vmem_limit_bytes=PEER_DENSE_VMEM_BYTES),
        name="peer_dense",
    )(ht, u_bf16, vt_chunks, p0s, p1, theta, xf, g2)


def _pad_cols(w, n):
    return jnp.pad(w, ((0, 0), (0, n - w.shape[1])))


def kernel(x, c, mod_w, mod_b, gla_w_in, gla_w_gate2, gla_b_gate, gla_o_norm, gla_w_out, kv_mod_w, kv_mod_b, fox_w_kvf, fox_b_f, fox_k_norm, fox_w_qg, fox_q_norm, fox_w_out, peer_w_q, peer_sub_keys, peer_u, peer_v):
    bsz, seq, d = x.shape
    depth = mod_w.shape[0]
    n_gla = gla_w_in.shape[0]
    xf = x.reshape(bsz * seq, d)
    gla_dk = GLA_HEADS * GLA_HK
    gla_main = 2 * gla_dk + 2 * GLA_HEADS * GLA_HV

    shared = None
    for layer in range(depth):
        if layer == n_gla:
            kv_mod = mod_matmul(c, kv_mod_w, kv_mod_b).reshape(bsz, 2, 1, d)
            kv_sh, kv_sc = kv_mod[:, 0], kv_mod[:, 1]
            kv = adaln_matmul(xf, kv_sh, kv_sc, fox_w_kvf[:, :2 * d].astype(BF16), seq)
            pos = jnp.arange(d) % LANES
            src = 2 * (jnp.arange(d) // LANES) + pos // FOX_BIAS_PIECES
            used = pos < FOX_BIAS_LANES
            sel = ((src[None, :] == jnp.arange(LANES)[:, None]) & used[None, :]).astype(F32)
            f = adaln_matmul(xf, kv_sh, kv_sc, _pad_cols(fox_w_kvf[:, 2 * d:], LANES), seq)
            bf_pad = jnp.pad(fox_b_f, (0, LANES - FOX_HEADS)).reshape(1, LANES)
            kb = forget_cumsum(f, bf_pad, sel, bsz, seq)
            head_avg = jnp.kron(jnp.eye(LANES // FOX_HD, dtype=F32), jnp.full((FOX_HD, FOX_HD), 1.0 / FOX_HD, F32))
            kn, vt = fox_kv_prep(kv, head_avg, jnp.tile(fox_k_norm, LANES // FOX_HD).reshape(1, LANES))
            shared = (kn, kb, vt, head_avg)

        mod = mod_matmul(c, mod_w[layer], mod_b[layer]).reshape(bsz, 6, 1, d)
        sh1, sc1, g1, sh2, sc2, g2 = (mod[:, i] for i in range(6))

        if layer < n_gla:
            w_in = gla_w_in[layer]
            proj = adaln_matmul(xf, sh1, sc1, w_in[:, :gla_main].astype(BF16), seq)
            glow = adaln_matmul(xf, sh1, sc1, _pad_cols(w_in[:, gla_main:], LANES), seq)
            w2p = jnp.pad(gla_w_gate2[layer], ((0, LANES - GLA_GATE_RANK), (0, 0)))
            o = gla_recurrence(proj, glow, w2p, gla_b_gate[layer].reshape(1, gla_dk), bsz, seq)
            xf = mixer_out(o, proj, (2 * gla_dk + GLA_HEADS * GLA_HV) // d, xf, g1,
                           gla_w_out[layer].astype(BF16), seq, o_norm=gla_o_norm[layer])
        else:
            j = layer - n_gla
            kn, kb, vt, head_avg = shared
            qg = adaln_matmul(xf, sh1, sc1, fox_w_qg[j].astype(BF16), seq)
            o = fox_attention(qg, kn, kb, vt, head_avg,
                              jnp.tile(fox_q_norm[j], LANES // FOX_HD).reshape(1, LANES), bsz, seq)
            xf = mixer_out(o, qg, 1, xf, g1, fox_w_out[j].astype(BF16), seq)

        sub_keys = peer_sub_keys[layer].reshape(2 * PEER_HEADS, PEER_NKEYS, -1)
        ht, p0s, p1, theta = peer_topk(xf, sh2, sc2, peer_w_q[layer].T, sub_keys, seq)
        vt_chunks = peer_v[layer].astype(BF16).reshape(-1, PEER_CHUNK, d).transpose(0, 2, 1)
        xf = peer_dense(ht, peer_u[layer].astype(BF16), vt_chunks, p0s, p1, theta, xf, g2, seq)
    return xf.reshape(bsz, seq, d)
```

```python
import functools
import math

import jax
import jax.numpy as jnp
from jax import lax
from jax.experimental import pallas as pl
from jax.experimental.pallas import tpu as pltpu

F32 = jnp.float32
BF16 = jnp.bfloat16
HI = lax.Precision.HIGHEST
EPS = 1e-6
NEG_INF = float("-inf")

VMEM_LIMIT_BYTES = 48 * 1024 * 1024
LANES = 128

GLA_HEADS = 4
GLA_HK = 128
GLA_HV = 256
GLA_GATE_RANK = 16
GLA_GATE_NORM = 16.0
GLA_CHUNK = 64

FOX_HD = 64
FOX_HEADS = 16

PEER_HEADS = 8
PEER_NKEYS = 128
PEER_TOPK = 16
PEER_CHUNK = 2048
PEER_DENSE_VMEM_BYTES = 56 * 1024 * 1024


def _params(*sem):
    return pltpu.CompilerParams(dimension_semantics=sem, vmem_limit_bytes=VMEM_LIMIT_BYTES)


def _adaln_rows(x, shift, scale):
    ms = jnp.mean(x * x, axis=-1, keepdims=True)
    return x * lax.rsqrt(ms + EPS) * (1.0 + scale) + shift


def _mod_kernel(c_ref, w_ref, b_ref, o_ref):
    o_ref[...] = jnp.dot(c_ref[...], w_ref[...], preferred_element_type=F32, precision=HI) + b_ref[...]


def mod_matmul(c, w, b, tn=512):
    bsz, d = c.shape
    n = w.shape[1]
    return pl.pallas_call(
        _mod_kernel,
        grid=(n // tn,),
        in_specs=[
            pl.BlockSpec((bsz, d), lambda j: (0, 0)),
            pl.BlockSpec((d, tn), lambda j: (0, j)),
            pl.BlockSpec((1, tn), lambda j: (0, j)),
        ],
        out_specs=pl.BlockSpec((bsz, tn), lambda j: (0, j)),
        out_shape=jax.ShapeDtypeStruct((bsz, n), F32),
        compiler_params=_params("parallel"),
        name="mod_matmul",
    )(c, w, b.reshape(1, n))


def _adaln_mm_kernel(x_ref, sh_ref, sc_ref, w_ref, o_ref, h_ref):
    @pl.when(pl.program_id(1) == 0)
    def _():
        h_ref[...] = _adaln_rows(x_ref[...], sh_ref[0], sc_ref[0]).astype(h_ref.dtype)

    if w_ref.dtype == BF16:
        acc = jnp.dot(h_ref[...], w_ref[...], preferred_element_type=F32)
    else:
        acc = jnp.dot(h_ref[...], w_ref[...], preferred_element_type=F32, precision=HI)
    o_ref[...] = acc.astype(o_ref.dtype)


def adaln_matmul(xf, shift, scale, w, seq, tm=512, tn=1024):
    t, d = xf.shape
    n = w.shape[1]
    tn = min(tn, n)
    per_seq = seq // tm
    return pl.pallas_call(
        _adaln_mm_kernel,
        grid=(t // tm, n // tn),
        in_specs=[
            pl.BlockSpec((tm, d), lambda i, j: (i, 0)),
            pl.BlockSpec((1, 1, d), lambda i, j: (i // per_seq, 0, 0)),
            pl.BlockSpec((1, 1, d), lambda i, j: (i // per_seq, 0, 0)),
            pl.BlockSpec((d, tn), lambda i, j: (0, j)),
        ],
        out_specs=pl.BlockSpec((tm, tn), lambda i, j: (i, j)),
        out_shape=jax.ShapeDtypeStruct((t, n), w.dtype),
        scratch_shapes=[pltpu.VMEM((tm, d), w.dtype)],
        compiler_params=_params("parallel", "arbitrary"),
        name="adaln_matmul",
    )(xf, shift, scale, w)


GLA_SUB = 16


def _gla_kernel(q_ref, k_ref, v_ref, gl_ref, w2_ref, bg_ref, o_ref, st_ref, *, nchunk):
    c, sb = GLA_CHUNK, GLA_SUB
    nt = (((1,), (1,)), ((), ()))

    @pl.when(pl.program_id(2) == 0)
    def _():
        st_ref[...] = jnp.zeros_like(st_ref)

    row_c = lax.broadcasted_iota(jnp.int32, (c, GLA_HK), 0)
    row_b = lax.broadcasted_iota(jnp.int32, (sb, GLA_HK), 0)
    lane_b = lax.broadcasted_iota(jnp.int32, (sb, GLA_HK), 1)
    tri = (lax.broadcasted_iota(jnp.int32, (c, c), 1) <= lax.broadcasted_iota(jnp.int32, (c, c), 0)).astype(F32)

    def head_chunk(r0, hh):
        kc = slice(hh * GLA_HK, (hh + 1) * GLA_HK)
        vc = slice(hh * GLA_HV, (hh + 1) * GLA_HV)
        q = q_ref[pl.ds(r0, c), kc].astype(F32) * (GLA_HK ** -0.5)
        k = k_ref[pl.ds(r0, c), kc].astype(F32)
        v = v_ref[pl.ds(r0, c), vc].astype(BF16)
        pre = jnp.dot(gl_ref[pl.ds(r0, c), :], w2_ref[:, kc], preferred_element_type=F32, precision=HI) + bg_ref[:, kc]
        gk = jax.nn.log_sigmoid(pre) * (1.0 / GLA_GATE_NORM)
        b = jnp.dot(tri, gk, preferred_element_type=F32, precision=HI)
        st = st_ref[hh]
        o_inter = lax.dot_general((q * jnp.exp(b)).astype(BF16), st.astype(BF16), nt, preferred_element_type=F32)

        blocks = []
        for i in range(c // sb):
            lo = i * sb
            b_i, q_i, k_i = b[lo:lo + sb], q[lo:lo + sb], k[lo:lo + sb]
            d = jnp.zeros((sb, GLA_HK), F32)
            for j in range(sb):
                rel = jnp.where(row_b >= j, b_i - b_i[j:j + 1], NEG_INF)
                m = jnp.exp(rel) * (q_i * k_i[j:j + 1])
                d = jnp.where(lane_b == lo + j, jnp.sum(m, axis=-1, keepdims=True), d)
            att = d[:, :c]
            if i > 0:
                ref = b_i[0:1]
                qs = q_i * jnp.exp(b_i - ref)
                ks = k * jnp.exp(jnp.where(row_c < lo, ref - b, NEG_INF))
                att = att + lax.dot_general(qs.astype(BF16), ks.astype(BF16), nt, preferred_element_type=F32)
            blocks.append(att)
        attn = jnp.concatenate(blocks, axis=0)
        o_ref[pl.ds(r0, c), vc] = o_inter + jnp.dot(attn.astype(BF16), v, preferred_element_type=F32)
        b_last = b[c - 1:c, :]
        kdec = (k * jnp.exp(b_last - b)).astype(BF16)
        st_ref[hh] = st * jnp.exp(b_last) + lax.dot_general(
            v, kdec, (((0,), (0,)), ((), ())), preferred_element_type=F32)

    def chunk(ci, carry):
        r0 = pl.multiple_of(ci * c, c)
        for hh in range(st_ref.shape[0]):
            head_chunk(r0, hh)
        return carry

    lax.fori_loop(0, nchunk, chunk, 0)


def gla_recurrence(proj, glow, w2p, bg, bsz, seq, ts=512, heads_per_step=4):
    t = proj.shape[0]
    ns = seq // ts
    hs = heads_per_step
    wk, wv = hs * GLA_HK, hs * GLA_HV
    kcol = (GLA_HEADS * GLA_HK) // wk
    vcol = (2 * GLA_HEADS * GLA_HK) // wv
    return pl.pallas_call(
        functools.partial(_gla_kernel, nchunk=ts // GLA_CHUNK),
        grid=(bsz, GLA_HEADS // hs, ns),
        in_specs=[
            pl.BlockSpec((ts, wk), lambda b, h, s: (b * ns + s, h)),
            pl.BlockSpec((ts, wk), lambda b, h, s: (b * ns + s, kcol + h)),
            pl.BlockSpec((ts, wv), lambda b, h, s: (b * ns + s, vcol + h)),
            pl.BlockSpec((ts, LANES), lambda b, h, s: (b * ns + s, 0)),
            pl.BlockSpec((LANES, wk), lambda b, h, s: (0, h)),
            pl.BlockSpec((1, wk), lambda b, h, s: (0, h)),
        ],
        out_specs=pl.BlockSpec((ts, wv), lambda b, h, s: (b * ns + s, h)),
        out_shape=jax.ShapeDtypeStruct((t, GLA_HEADS * GLA_HV), F32),
        scratch_shapes=[pltpu.VMEM((hs, GLA_HV, GLA_HK), F32)],
        compiler_params=_params("parallel", "parallel", "arbitrary"),
        name="gla_recurrence",
    )(proj, proj, proj, glow, w2p, bg)


def _gla_out_kernel(o_ref, g_ref, x_ref, g1_ref, on_ref, w_ref, out_ref):
    parts = []
    for h in range(GLA_HEADS):
        oh = o_ref[:, h * GLA_HV:(h + 1) * GLA_HV]
        ms = jnp.mean(oh * oh, axis=-1, keepdims=True)
        parts.append(oh * lax.rsqrt(ms + EPS) * on_ref[...])
    y = jnp.concatenate(parts, axis=-1) * jax.nn.silu(g_ref[...].astype(F32))
    mix = jnp.dot(y.astype(BF16), w_ref[...], preferred_element_type=F32)
    out_ref[...] = x_ref[...] + g1_ref[0] * mix


def _fox_out_kernel(o_ref, g_ref, x_ref, g1_ref, w_ref, out_ref):
    y = o_ref[...] * jax.nn.sigmoid(g_ref[...].astype(F32))
    mix = jnp.dot(y.astype(BF16), w_ref[...], preferred_element_type=F32)
    out_ref[...] = x_ref[...] + g1_ref[0] * mix


def mixer_out(o, gsrc, gcol, xf, g1, w_bf16, seq, o_norm=None, tm=512):
    t, d = xf.shape
    per_seq = seq // tm
    row = lambda i: (i, 0)
    in_specs = [
        pl.BlockSpec((tm, d), row),
        pl.BlockSpec((tm, d), lambda i: (i, gcol)),
        pl.BlockSpec((tm, d), row),
        pl.BlockSpec((1, 1, d), lambda i: (i // per_seq, 0, 0)),
    ]
    args = [o, gsrc, xf, g1]
    if o_norm is not None:
        in_specs.append(pl.BlockSpec((1, GLA_HV), lambda i: (0, 0)))
        args.append(o_norm.reshape(1, GLA_HV))
        body = _gla_out_kernel
    else:
        body = _fox_out_kernel
    in_specs.append(pl.BlockSpec((d, d), lambda i: (0, 0)))
    args.append(w_bf16)
    return pl.pallas_call(
        body,
        grid=(t // tm,),
        in_specs=in_specs,
        out_specs=pl.BlockSpec((tm, d), row),
        out_shape=jax.ShapeDtypeStruct((t, d), F32),
        compiler_params=_params("parallel"),
        name="mixer_out",
    )(*args)


FOX_BIAS_PIECES = 3
FOX_BIAS_LANES = 2 * FOX_BIAS_PIECES


def _fgate_kernel(f_ref, bf_ref, sel_ref, kb_ref, carry_ref):
    tc = f_ref.shape[0]

    @pl.when(pl.program_id(1) == 0)
    def _():
        carry_ref[...] = jnp.zeros_like(carry_ref)

    logf = jax.nn.log_sigmoid(f_ref[...] + bf_ref[...])
    tri = (lax.broadcasted_iota(jnp.int32, (tc, tc), 1) <= lax.broadcasted_iota(jnp.int32, (tc, tc), 0)).astype(F32)
    cum_heads = jnp.dot(tri, logf, preferred_element_type=F32, precision=HI) + carry_ref[...]
    carry_ref[...] = cum_heads[tc - 1:tc, :]
    cum = jnp.dot(cum_heads, sel_ref[...], preferred_element_type=F32, precision=HI)
    hi = cum.astype(BF16).astype(F32)
    mid = (cum - hi).astype(BF16).astype(F32)
    lo = cum - hi - mid
    pos = lax.broadcasted_iota(jnp.int32, cum.shape, 1) % LANES
    piece = pos % FOX_BIAS_PIECES
    out = jnp.where(piece == 0, hi, jnp.where(piece == 1, mid, lo))
    kb_ref[...] = jnp.where(pos < FOX_BIAS_LANES, out, 0.0).astype(BF16)


def forget_cumsum(f, bf_pad, sel, bsz, seq, tc=256):
    t, nf = f.shape
    n = sel.shape[1]
    ns = seq // tc
    return pl.pallas_call(
        _fgate_kernel,
        grid=(bsz, ns),
        in_specs=[
            pl.BlockSpec((tc, nf), lambda b, s: (b * ns + s, 0)),
            pl.BlockSpec((1, nf), lambda b, s: (0, 0)),
            pl.BlockSpec((nf, n), lambda b, s: (0, 0)),
        ],
        out_specs=pl.BlockSpec((tc, n), lambda b, s: (b * ns + s, 0)),
        out_shape=jax.ShapeDtypeStruct((t, n), BF16),
        scratch_shapes=[pltpu.VMEM((1, nf), F32)],
        compiler_params=_params("parallel", "arbitrary"),
        name="forget_cumsum",
    )(f, bf_pad, sel)


def _pair_headnorm(x2, bd, gain2):
    ms = jnp.dot(x2 * x2, bd, preferred_element_type=F32, precision=HI)
    return x2 * lax.rsqrt(ms + EPS) * gain2


def _kv_prep_kernel(kv_ref, bd_ref, kn_ref, ko_ref, vt_ref):
    d = ko_ref.shape[1]
    for cb in range(d // LANES):
        cols = slice(cb * LANES, (cb + 1) * LANES)
        ko_ref[:, cols] = _pair_headnorm(kv_ref[:, cols].astype(F32), bd_ref[...], kn_ref[...]).astype(BF16)
        vt_ref[cols, :] = kv_ref[:, d + cb * LANES:d + (cb + 1) * LANES].astype(F32).T.astype(BF16)


def fox_kv_prep(kv, bd, k_gain2, tm=512):
    t, d2 = kv.shape
    d = d2 // 2
    return pl.pallas_call(
        _kv_prep_kernel,
        grid=(t // tm,),
        in_specs=[
            pl.BlockSpec((tm, d2), lambda i: (i, 0)),
            pl.BlockSpec((LANES, LANES), lambda i: (0, 0)),
            pl.BlockSpec((1, LANES), lambda i: (0, 0)),
        ],
        out_specs=[pl.BlockSpec((tm, d), lambda i: (i, 0)), pl.BlockSpec((d, tm), lambda i: (0, i))],
        out_shape=[jax.ShapeDtypeStruct((t, d), BF16), jax.ShapeDtypeStruct((d, t), BF16)],
        compiler_params=_params("parallel"),
        name="fox_kv_prep",
    )(kv, bd, k_gain2)


def _fox_attn_kernel(qi_ref, ki_ref, q_ref, k_ref, kb_ref, vt_ref, bd_ref, qn_ref, o_ref,
                     qs_ref, m_ref, l_ref, acc_ref):
    qi = qi_ref[pl.program_id(2)]
    ki = ki_ref[pl.program_id(2)]
    tq = q_ref.shape[0]
    tk = k_ref.shape[0]
    npb = q_ref.shape[1] // LANES

    @pl.when(ki == 0)
    def _():
        for pr in range(npb):
            cols = slice(pr * LANES, (pr + 1) * LANES)
            qt = (_pair_headnorm(q_ref[:, cols].astype(F32), bd_ref[...], qn_ref[...]) * (FOX_HD ** -0.5)).T
            row = lax.broadcasted_iota(jnp.int32, qt.shape, 0)
            for j in range(2):
                top = jnp.where((row >= j * FOX_HD) & (row < (j + 1) * FOX_HD), qt, 0.0)
                lo = j * FOX_BIAS_PIECES
                bias = jnp.where((row >= lo) & (row < lo + FOX_BIAS_PIECES), -1.0, 0.0)
                qs_ref[2 * pr + j] = jnp.concatenate([top, bias], axis=0).astype(BF16)
        m_ref[...] = jnp.full_like(m_ref, NEG_INF)
        l_ref[...] = jnp.zeros_like(l_ref)
        acc_ref[...] = jnp.zeros_like(acc_ref)

    def update(diagonal):
        if diagonal:
            causal = (lax.broadcasted_iota(jnp.int32, (tk, tq), 0) <= lax.broadcasted_iota(jnp.int32, (tk, tq), 1))
        for pr in range(npb):
            cols = slice(pr * LANES, (pr + 1) * LANES)
            kext = jnp.concatenate([k_ref[:, cols], kb_ref[:, cols]], axis=1)
            vt = vt_ref[cols, :]
            for j in range(2):
                h = 2 * pr + j
                st = jnp.dot(kext, qs_ref[h], preferred_element_type=F32)
                if diagonal:
                    st = jnp.where(causal, st, NEG_INF)
                m_prev = m_ref[h]
                m_new = jnp.maximum(m_prev, jnp.max(st, axis=0, keepdims=True))
                alpha = jnp.exp(m_prev - m_new)
                p = jnp.exp(st - m_new)
                l_ref[h] = alpha * l_ref[h] + jnp.sum(p, axis=0, keepdims=True)
                acc_ref[h] = alpha * acc_ref[h] + jnp.dot(vt, p.astype(BF16), preferred_element_type=F32)
                m_ref[h] = m_new

    @pl.when(ki < qi)
    def _():
        update(False)

    @pl.when(ki == qi)
    def _():
        update(True)
        row = lax.broadcasted_iota(jnp.int32, (LANES, tq), 0)
        for pr in range(npb):
            h0, h1 = 2 * pr, 2 * pr + 1
            o_ref[:, pr * LANES:(pr + 1) * LANES] = jnp.where(
                row < FOX_HD, acc_ref[h0] / l_ref[h0], acc_ref[h1] / l_ref[h1]).T


def fox_attention(qg, kn, kb, vt, bd, q_gain2, bsz, seq, tq=512, heads_per_step=8):
    t = qg.shape[0]
    d = FOX_HEADS * FOX_HD
    nq = seq // tq
    hs = heads_per_step
    wcol = hs * FOX_HD
    pairs = [(i, j) for i in range(nq) for j in range(i + 1)]
    qi_tab = jnp.asarray([p[0] for p in pairs], jnp.int32)
    ki_tab = jnp.asarray([p[1] for p in pairs], jnp.int32)
    qblk = lambda b, h, s, qt, kt: (b * nq + qt[s], h)
    kblk = lambda b, h, s, qt, kt: (b * nq + kt[s], h)
    const = lambda b, h, s, qt, kt: (0, 0)
    grid_spec = pltpu.PrefetchScalarGridSpec(
        num_scalar_prefetch=2,
        grid=(bsz, FOX_HEADS // hs, len(pairs)),
        in_specs=[
            pl.BlockSpec((tq, wcol), qblk),
            pl.BlockSpec((tq, wcol), kblk),
            pl.BlockSpec((tq, wcol), kblk),
            pl.BlockSpec((wcol, tq), lambda b, h, s, qt, kt: (h, b * nq + kt[s])),
            pl.BlockSpec((LANES, LANES), const),
            pl.BlockSpec((1, LANES), const),
        ],
        out_specs=pl.BlockSpec((tq, wcol), qblk),
        scratch_shapes=[
            pltpu.VMEM((hs, 2 * LANES, tq), BF16),
            pltpu.VMEM((hs, 1, tq), F32),
            pltpu.VMEM((hs, 1, tq), F32),
            pltpu.VMEM((hs, LANES, tq), F32),
        ],
    )
    return pl.pallas_call(
        _fox_attn_kernel,
        grid_spec=grid_spec,
        out_shape=jax.ShapeDtypeStruct((t, d), F32),
        compiler_params=_params("parallel", "parallel", "arbitrary"),
        name="fox_attention",
    )(qi_tab, ki_tab, qg, kn, kb, vt, bd, q_gain2)


SUBLANES = 8


def _merge_desc(v):
    n = len(v)
    if n == 1:
        return v
    half = n // 2
    hi = [jnp.maximum(v[i], v[i + half]) for i in range(half)]
    lo = [jnp.minimum(v[i], v[i + half]) for i in range(half)]
    return _merge_desc(hi) + _merge_desc(lo)


def _sort_desc(v):
    n = len(v)
    if n == 1:
        return v
    return _merge_desc(_sort_desc(v[:n // 2]) + _sort_desc(v[n // 2:])[::-1])


def _top16_values(groups):
    lists = _sort_desc(groups)
    shift = SUBLANES // 2
    while shift >= 1:
        partner = [pltpu.roll(a, shift, 0) for a in lists]
        if len(lists) < PEER_TOPK:
            lists = _merge_desc(lists + partner[::-1])
        else:
            n = len(lists)
            lists = _merge_desc([jnp.maximum(lists[i], partner[n - 1 - i]) for i in range(n)])
        shift //= 2
    return lists


_CAND_PAIRS = [(i, j) for i in range(PEER_TOPK) for j in range(PEER_TOPK) if (i + 1) * (j + 1) <= PEER_TOPK]


def _split_bf16(x):
    hi = x.astype(BF16)
    return hi, (x - hi.astype(F32)).astype(BF16)


def _peer_topk_kernel(x_ref, sh_ref, sc_ref, wqh_ref, wql_ref, sk_ref, ht_ref, p0s_ref, p1_ref, th_ref, qt_ref):
    h = _adaln_rows(x_ref[...], sh_ref[0], sc_ref[0])
    ht = h.T
    ht_ref[...] = ht.astype(BF16)
    dq = PEER_NKEYS
    ngroups = PEER_NKEYS // SUBLANES
    tb = ht.shape[1]
    sub = lax.broadcasted_iota(jnp.int32, (SUBLANES, tb), 0)
    thetas = []
    ht_hi, ht_lo = _split_bf16(ht)
    qt_ref[...] = (jnp.dot(wqh_ref[...], ht_hi, preferred_element_type=F32)
                   + jnp.dot(wqh_ref[...], ht_lo, preferred_element_type=F32)
                   + jnp.dot(wql_ref[...], ht_hi, preferred_element_type=F32))
    for head in range(PEER_HEADS):
        groups, tops = [], []
        for half in range(2):
            hp = 2 * head + half
            st = jnp.dot(sk_ref[hp], qt_ref[hp * dq:(hp + 1) * dq, :],
                         preferred_element_type=F32, precision=HI)
            g = [st[SUBLANES * k:SUBLANES * (k + 1), :] for k in range(ngroups)]
            m = functools.reduce(jnp.maximum, g)
            for shift in (4, 2, 1):
                m = jnp.maximum(m, pltpu.roll(m, shift, 0))
            g = [jnp.exp(v - m) for v in g]
            groups.append(g)
            tops.append(_top16_values(g))
        p0, p1 = tops

        def top_products(a, b):
            packed = []
            for k in range(0, len(_CAND_PAIRS), SUBLANES):
                acc = jnp.zeros((SUBLANES, tb), F32)
                for s_, (i, j) in enumerate(_CAND_PAIRS[k:k + SUBLANES]):
                    acc = jnp.where(sub == s_, a[i] * b[j], acc)
                packed.append(acc)
            while len(packed) & (len(packed) - 1):
                packed.append(jnp.zeros((SUBLANES, tb), F32))
            return _top16_values(packed)

        inv_z = 1.0 / functools.reduce(jnp.add, top_products(p0, p1))
        p0s = [v * inv_z for v in p0]
        thetas.append(top_products(p0s, p1)[PEER_TOPK - 1][0:1, :])
        for half, ref in enumerate((p0s_ref, p1_ref)):
            t16 = tops[half][PEER_TOPK - 1]
            for k in range(ngroups):
                g = groups[half][k]
                tab = jnp.where(g >= t16, g, 0.0)
                ref[head, SUBLANES * k:SUBLANES * (k + 1), :] = tab * inv_z if half == 0 else tab
    th_ref[...] = jnp.concatenate(thetas, axis=0)


def peer_topk(xf, shift, scale, wq_t, sub_keys, seq, tb=256):
    wq_hi, wq_lo = _split_bf16(wq_t)
    t, d = xf.shape
    per_seq = seq // tb
    nh, nk = PEER_HEADS, PEER_NKEYS
    tab = jax.ShapeDtypeStruct((nh, nk, t), F32)
    tab_spec = pl.BlockSpec((nh, nk, tb), lambda i: (0, 0, i))
    return pl.pallas_call(
        _peer_topk_kernel,
        grid=(t // tb,),
        in_specs=[
            pl.BlockSpec((tb, d), lambda i: (i, 0)),
            pl.BlockSpec((1, 1, d), lambda i: (i // per_seq, 0, 0)),
            pl.BlockSpec((1, 1, d), lambda i: (i // per_seq, 0, 0)),
            pl.BlockSpec(wq_t.shape, lambda i: (0, 0)),
            pl.BlockSpec(wq_t.shape, lambda i: (0, 0)),
            pl.BlockSpec(sub_keys.shape, lambda i: (0, 0, 0)),
        ],
        out_specs=[
            pl.BlockSpec((d, tb), lambda i: (0, i)),
            tab_spec, tab_spec,
            pl.BlockSpec((nh, tb), lambda i: (0, i)),
        ],
        out_shape=[jax.ShapeDtypeStruct((d, t), BF16), tab, tab, jax.ShapeDtypeStruct((nh, t), F32)],
        scratch_shapes=[pltpu.VMEM((wq_t.shape[0], tb), F32)],
        compiler_params=_params("parallel"),
        name="peer_topk",
    )(xf, shift, scale, wq_hi, wq_lo, sub_keys)


def _peer_dense_kernel(ht_ref, u_ref, vt_ref, p0s_ref, p1_ref, th_ref, x_ref, g2_ref, out_ref,
                       at_ref, wt_ref, acc_ref, *, tiles):
    e = pl.program_id(1)
    nk = PEER_NKEYS
    slot = e % 2
    prev = 1 - slot
    nsub, _, sub = wt_ref.shape

    def u_matmul(s):
        return jnp.dot(u_ref[...], ht_ref[:, s * sub:(s + 1) * sub], preferred_element_type=F32)

    @pl.when(e == 0)
    def _():
        acc_ref[...] = jnp.zeros_like(acc_ref)
        for s in range(nsub):
            at_ref[0, s] = u_matmul(s)

    @pl.when(e > 0)
    def _():
        for s in range(nsub):
            cols = slice(s * sub, (s + 1) * sub)
            a_new = u_matmul(s)
            for i in range(tiles):
                rows = slice(i * nk, (i + 1) * nk)
                a = at_ref[prev, s, rows, :]
                act = 0.5 * a * (1.0 + lax.erf(a * (1.0 / math.sqrt(2.0))))
                g = None
                for head in range(PEER_HEADS):
                    w = p0s_ref[head, i:i + 1, cols] * p1_ref[head, :, cols]
                    sel = w >= th_ref[head:head + 1, cols]
                    g = jnp.where(sel, w, 0.0) if g is None else jnp.where(sel, g + w, g)
                wt_ref[s, rows, :] = (g * act).astype(BF16)
            acc_ref[s] += jnp.dot(vt_ref[0], wt_ref[s], preferred_element_type=F32)
            at_ref[slot, s] = a_new

    @pl.when(e == pl.num_programs(1) - 1)
    def _():
        for s in range(nsub):
            rows = slice(s * sub, (s + 1) * sub)
            out_ref[rows, :] = x_ref[rows, :] + g2_ref[0] * acc_ref[s].T


def peer_dense(ht, u_bf16, vt_chunks, p0s, p1, theta, xf, g2, seq, tb=512, sub=256):
    t, d = xf.shape
    nchunk, _, ec = vt_chunks.shape
    per_seq = seq // tb
    nh, nk = PEER_HEADS, PEER_NKEYS
    tiles = ec // nk
    return pl.pallas_call(
        functools.partial(_peer_dense_kernel, tiles=tiles),
        grid=(t // tb, nchunk + 1),
        in_specs=[
            pl.BlockSpec((d, tb), lambda i, e: (0, i)),
            pl.BlockSpec((ec, d), lambda i, e: (jnp.minimum(e, nchunk - 1), 0)),
            pl.BlockSpec((1, d, ec), lambda i, e: (jnp.maximum(e - 1, 0), 0, 0)),
            pl.BlockSpec((nh, tiles, tb), lambda i, e: (0, jnp.maximum(e - 1, 0), i)),
            pl.BlockSpec((nh, nk, tb), lambda i, e: (0, 0, i)),
            pl.BlockSpec((nh, tb), lambda i, e: (0, i)),
            pl.BlockSpec((tb, d), lambda i, e: (i, 0)),
            pl.BlockSpec((1, 1, d), lambda i, e: (i // per_seq, 0, 0)),
        ],
        out_specs=pl.BlockSpec((tb, d), lambda i, e: (i, 0)),
        out_shape=jax.ShapeDtypeStruct((t, d), F32),
        scratch_shapes=[
            pltpu.VMEM((2, tb // sub, ec, sub), F32),
            pltpu.VMEM((tb // sub, ec, sub), BF16),
            pltpu.VMEM((tb // sub, d, sub), F32),
        ],
        compiler_params=pltpu.CompilerParams(dimension_semantics=("parallel", "arbitrary"),
                                             vmem_limit_bytes=PEER_DENSE_VMEM_BYTES),
        name="peer_dense",
    )(ht, u_bf16, vt_chunks, p0s, p1, theta, xf, g2)


def _pad_cols(w, n):
    return jnp.pad(w, ((0, 0), (0, n - w.shape[1])))


def kernel(x, c, mod_w, mod_b, gla_w_in, gla_w_gate2, gla_b_gate, gla_o_norm, gla_w_out, kv_mod_w, kv_mod_b, fox_w_kvf, fox_b_f, fox_k_norm, fox_w_qg, fox_q_norm, fox_w_out, peer_w_q, peer_sub_keys, peer_u, peer_v):
    bsz, seq, d = x.shape
    depth = mod_w.shape[0]
    n_gla = gla_w_in.shape[0]
    xf = x.reshape(bsz * seq, d)
    gla_dk = GLA_HEADS * GLA_HK
    gla_main = 2 * gla_dk + 2 * GLA_HEADS * GLA_HV

    shared = None
    for layer in range(depth):
        if layer == n_gla:
            kv_mod = mod_matmul(c, kv_mod_w, kv_mod_b).reshape(bsz, 2, 1, d)
            kv_sh, kv_sc = kv_mod[:, 0], kv_mod[:, 1]
            kv = adaln_matmul(xf, kv_sh, kv_sc, fox_w_kvf[:, :2 * d].astype(BF16), seq)
            pos = jnp.arange(d) % LANES
            src = 2 * (jnp.arange(d) // LANES) + pos // FOX_BIAS_PIECES
            used = pos < FOX_BIAS_LANES
            sel = ((src[None, :] == jnp.arange(LANES)[:, None]) & used[None, :]).astype(F32)
            f = adaln_matmul(xf, kv_sh, kv_sc, _pad_cols(fox_w_kvf[:, 2 * d:], LANES), seq)
            bf_pad = jnp.pad(fox_b_f, (0, LANES - FOX_HEADS)).reshape(1, LANES)
            kb = forget_cumsum(f, bf_pad, sel, bsz, seq)
            head_avg = jnp.kron(jnp.eye(LANES // FOX_HD, dtype=F32), jnp.full((FOX_HD, FOX_HD), 1.0 / FOX_HD, F32))
            kn, vt = fox_kv_prep(kv, head_avg, jnp.tile(fox_k_norm, LANES // FOX_HD).reshape(1, LANES))
            shared = (kn, kb, vt, head_avg)

        mod = mod_matmul(c, mod_w[layer], mod_b[layer]).reshape(bsz, 6, 1, d)
        sh1, sc1, g1, sh2, sc2, g2 = (mod[:, i] for i in range(6))

        if layer < n_gla:
            w_in = gla_w_in[layer]
            proj = adaln_matmul(xf, sh1, sc1, w_in[:, :gla_main].astype(BF16), seq)
            glow = adaln_matmul(xf, sh1, sc1, _pad_cols(w_in[:, gla_main:], LANES), seq)
            w2p = jnp.pad(gla_w_gate2[layer], ((0, LANES - GLA_GATE_RANK), (0, 0)))
            o = gla_recurrence(proj, glow, w2p, gla_b_gate[layer].reshape(1, gla_dk), bsz, seq)
            xf = mixer_out(o, proj, (2 * gla_dk + GLA_HEADS * GLA_HV) // d, xf, g1,
                           gla_w_out[layer].astype(BF16), seq, o_norm=gla_o_norm[layer])
        else:
            j = layer - n_gla
            kn, kb, vt, head_avg = shared
            qg = adaln_matmul(xf, sh1, sc1, fox_w_qg[j].astype(BF16), seq)
            o = fox_attention(qg, kn, kb, vt, head_avg,
                              jnp.tile(fox_q_norm[j], LANES // FOX_HD).reshape(1, LANES), bsz, seq)
            xf = mixer_out(o, qg, 1, xf, g1, fox_w_out[j].astype(BF16), seq)

        sub_keys = peer_sub_keys[layer].reshape(2 * PEER_HEADS, PEER_NKEYS, -1)
        ht, p0s, p1, theta = peer_topk(xf, sh2, sc2, peer_w_q[layer].T, sub_keys, seq)
        vt_chunks = peer_v[layer].astype(BF16).reshape(-1, PEER_CHUNK, d).transpose(0, 2, 1)
        xf = peer_dense(ht, peer_u[layer].astype(BF16), vt_chunks, p0s, p1, theta, xf, g2, seq)
    return xf.reshape(bsz, seq, d)
```
